```python
import math
import jax, jax.numpy as jnp
from jax import lax
import numpy as np

D_MODEL = 2048
BATCH = 4
SEQ = 8192
DEPTH = 1
DEC_BATCH = 32
DEC_SEQ = 64
PAST_LEN = 4096

CHUNK = 64
PLE_DIM = 256
MIX_A = D_MODEL // 2
MIX_B = D_MODEL // 2
ML_HEADS = 4
ML_HEAD_DIM = MIX_A // ML_HEADS
SB_HEADS = 8
SB_HEAD_DIM = MIX_B // SB_HEADS
CONV_W = 4
D_FF = 4 * D_MODEL
SB_BLOCK = 128
EPS = 1e-6

OFF_QK = 2 * MIX_A
OFF_V = OFF_QK + MIX_A
OFF_O = OFF_V + MIX_A
OFF_IF = OFF_O + 2 * ML_HEADS
OFF_SB = OFF_IF + 3 * MIX_B
D_IN = OFF_SB + 2 * D_MODEL
SPLITS = (OFF_QK, OFF_V, OFF_O, OFF_IF, OFF_SB)

kernel_name = 'hybrid_mlstm_stickbreaking_stream_step'


def rmsnorm(x, g):
    xf = x.astype(jnp.float32)
    y = xf * lax.rsqrt(jnp.mean(xf * xf, axis=-1, keepdims=True) + EPS)
    return (y * g.astype(jnp.float32)).astype(x.dtype)


def causal_conv(u, buf, w, b):
    ext = jnp.concatenate([buf.astype(u.dtype), u], axis=1)
    out = lax.conv_general_dilated(ext, w.astype(u.dtype)[:, None, :], (1,), 'VALID',
                                   dimension_numbers=('NWC', 'WIO', 'NWC'),
                                   feature_group_count=u.shape[-1])
    return out + b.astype(u.dtype), ext[:, -(CONV_W - 1):]


def mlstm_chunk(carry, xs):
    C, n, m = carry
    q, k, v, ig, lf = xs
    L = q.shape[1]
    b = jnp.cumsum(lf, axis=1)
    causal = jnp.tril(jnp.ones((L, L), dtype=bool))
    dmat = b[:, :, None, :] - b[:, None, :, :] + ig[:, None, :, :]
    dmat = jnp.where(causal[None, :, :, None], dmat, -jnp.inf)
    inter = b + m[:, None, :]
    m_t = jnp.maximum(inter, jnp.max(dmat, axis=2))
    wts = jnp.exp(dmat - m_t[:, :, None, :])
    s_qk = jnp.einsum('bthd,bshd->btsh', q, k) * wts
    dec = jnp.exp(inter - m_t)
    num = jnp.einsum('btsh,bshd->bthd', s_qk, v) + dec[..., None] * jnp.einsum('bthk,bhkv->bthv', q, C)
    den = jnp.sum(s_qk, axis=2) + dec * jnp.einsum('bthk,bhk->bth', q, n)
    h = num / jnp.maximum(jnp.abs(den), jnp.exp(-m_t))[..., None]
    m_last = m_t[:, -1]
    b_last = b[:, -1]
    wk = jnp.exp(b_last[:, None, :] - b + ig - m_last[:, None, :])
    carry_dec = jnp.exp(b_last + m - m_last)
    C_new = carry_dec[..., None, None] * C + jnp.einsum('bsh,bshk,bshv->bhkv', wk, k, v)
    n_new = carry_dec[..., None] * n + jnp.einsum('bsh,bshk->bhk', wk, k)
    return (C_new, n_new, m_last), h


def mlstm_seq(q, k, v, ig, lf, C0, n0, m0):
    B, L, H, d = q.shape
    cl = min(L, CHUNK)
    nc = L // cl
    to_chunks = lambda t: t.astype(jnp.float32).reshape((B, nc, cl) + t.shape[2:]).swapaxes(0, 1)
    xs = (to_chunks(q), to_chunks(k), to_chunks(v), to_chunks(ig), to_chunks(lf))
    init = (C0.astype(jnp.float32), n0.astype(jnp.float32), m0.astype(jnp.float32))
    (C, n, m), h = lax.scan(mlstm_chunk, init, xs)
    return h.swapaxes(0, 1).reshape(B, L, H, d), C, n, m


def sb_block(q, q_pos, k, v, k_pos):
    d = q.shape[-1]
    z = jnp.einsum('bthd,bshd->bhts', q.astype(jnp.float32), k.astype(jnp.float32)) * (d ** -0.5)
    mask = (k_pos[None, :] < q_pos[:, None])[None, None]
    log_beta = jax.nn.log_sigmoid(z)
    log_keep = jnp.where(mask, log_beta - z, 0.0)
    later = lax.cumsum(log_keep, axis=3, reverse=True) - log_keep
    a = jnp.where(mask, jnp.exp(log_beta + later), 0.0)
    return jnp.einsum('bhts,bshd->bthd', a, v.astype(jnp.float32))


def stick_breaking(q, k, v, past):
    B, L, H, d = q.shape
    qb = min(L, SB_BLOCK)
    nb = L // qb
    k_pos = jnp.arange(k.shape[1])
    q_pos = (past + jnp.arange(L)).reshape(nb, qb)
    q_blocks = q.reshape(B, nb, qb, H, d).swapaxes(0, 1)
    out = lax.map(lambda a: sb_block(a[0], a[1], k, v, k_pos), (q_blocks, q_pos))
    return out.swapaxes(0, 1).reshape(B, L, H, d)


def hybrid_layer(x, p, conv_buf, C0, n0, m0, k_past, v_past,
                 w_in, b_if, conv_w, conv_b, ml_norm, w_br_a, w_br_b, w_out,
                 g_pre_mix, g_post_mix, g_pre_mlp, g_post_mlp, w_up, w_down,
                 g_pre_ple, g_post_ple, w_ple, w_ple_gate):
    B, L, _ = x.shape
    past = k_past.shape[1]
    h = rmsnorm(x, g_pre_mix)
    proj = h @ w_in.astype(x.dtype)
    qk_pre, v_m, o_pre, if_pre, qkv_s, gate_pre = jnp.split(proj, SPLITS, axis=-1)
    qk, conv_new = causal_conv(qk_pre, conv_buf, conv_w, conv_b)
    qk = jax.nn.silu(qk)
    q_m = qk[..., :MIX_A].reshape(B, L, ML_HEADS, ML_HEAD_DIM)
    k_m = qk[..., MIX_A:].reshape(B, L, ML_HEADS, ML_HEAD_DIM) * (ML_HEAD_DIM ** -0.5)
    v_m = v_m.reshape(B, L, ML_HEADS, ML_HEAD_DIM)
    ifp = if_pre.astype(jnp.float32) + b_if.astype(jnp.float32)
    ig = ifp[..., :ML_HEADS]
    lf = jax.nn.log_sigmoid(ifp[..., ML_HEADS:])
    h_t, C, n, m = mlstm_seq(q_m, k_m, v_m, ig, lf, C0, n0, m0)
    h_m = jax.nn.sigmoid(o_pre).reshape(B, L, ML_HEADS, ML_HEAD_DIM) * h_t.astype(x.dtype)
    h_m = rmsnorm(h_m, ml_norm.reshape(ML_HEADS, ML_HEAD_DIM)).reshape(B, L, MIX_A)
    q_s, k_s, v_s = [t.reshape(B, L, SB_HEADS, SB_HEAD_DIM) for t in jnp.split(qkv_s, 3, axis=-1)]
    k_all = jnp.concatenate([k_past.astype(k_s.dtype), k_s], axis=1)
    v_all = jnp.concatenate([v_past.astype(v_s.dtype), v_s], axis=1)
    h_s = stick_breaking(q_s, k_all, v_all, past).reshape(B, L, MIX_B).astype(x.dtype)
    g_a, g_b = jnp.split(jax.nn.sigmoid(gate_pre), 2, axis=-1)
    u = g_a * (h_m @ w_br_a.astype(x.dtype)) + g_b * (h_s @ w_br_b.astype(x.dtype))
    x = x + rmsnorm(u @ w_out.astype(x.dtype), g_post_mix)
    h2 = rmsnorm(x, g_pre_mlp)
    f = jnp.square(jax.nn.relu(h2 @ w_up.astype(x.dtype))) @ w_down.astype(x.dtype)
    x = x + rmsnorm(f, g_post_mlp)
    gate = jax.nn.sigmoid(rmsnorm(x, g_pre_ple) @ w_ple_gate.astype(x.dtype))
    ple = (p.astype(x.dtype) @ w_ple.astype(x.dtype)) * gate
    x = x + rmsnorm(ple, g_post_ple)
    return x, (k_s, v_s, conv_new, C, n, m)


def setup_inputs(seed: int = 0) -> dict:
    key = jax.random.key(seed)
    ks = iter(jax.random.split(key, 40))
    nrm = lambda shape, scale: jax.random.normal(next(ks), shape, jnp.float32) * scale
    gain = lambda width: 1.0 + nrm((DEPTH, width), 0.02)
    b_if = jnp.concatenate([nrm((DEPTH, ML_HEADS), 0.1),
                            jnp.linspace(3.0, 6.0, ML_HEADS)[None, :] + nrm((DEPTH, ML_HEADS), 0.01)], axis=-1)
    return {
        'x_prompt': nrm((BATCH, SEQ, D_MODEL), 1.0),
        'x_sample': nrm((DEC_BATCH, DEC_SEQ, D_MODEL), 1.0),
        'p_prompt': nrm((DEPTH, BATCH, SEQ, PLE_DIM), 1.0),
        'p_sample': nrm((DEPTH, DEC_BATCH, DEC_SEQ, PLE_DIM), 1.0),
        'cache_sb_k': nrm((DEPTH, DEC_BATCH, PAST_LEN, SB_HEADS, SB_HEAD_DIM), 1.0),
        'cache_sb_v': nrm((DEPTH, DEC_BATCH, PAST_LEN, SB_HEADS, SB_HEAD_DIM), 1.0),
        'state_conv': nrm((DEPTH, DEC_BATCH, CONV_W - 1, 2 * MIX_A), 1.0),
        'state_mlstm_C': nrm((DEPTH, DEC_BATCH, ML_HEADS, ML_HEAD_DIM, ML_HEAD_DIM), 0.3),
        'state_mlstm_n': nrm((DEPTH, DEC_BATCH, ML_HEADS, ML_HEAD_DIM), 0.3),
        'state_mlstm_m': nrm((DEPTH, DEC_BATCH, ML_HEADS), 1.0),
        'w_in': nrm((DEPTH, D_MODEL, D_IN), D_MODEL ** -0.5),
        'b_if': b_if,
        'conv_w': nrm((DEPTH, CONV_W, 2 * MIX_A), CONV_W ** -0.5),
        'conv_b': nrm((DEPTH, 2 * MIX_A), 0.01),
        'ml_norm': gain(MIX_A),
        'w_br_a': nrm((DEPTH, MIX_A, D_MODEL), MIX_A ** -0.5),
        'w_br_b': nrm((DEPTH, MIX_B, D_MODEL), MIX_B ** -0.5),
        'w_out': nrm((DEPTH, D_MODEL, D_MODEL), D_MODEL ** -0.5),
        'g_pre_mix': gain(D_MODEL),
        'g_post_mix': gain(D_MODEL),
        'g_pre_mlp': gain(D_MODEL),
        'g_post_mlp': gain(D_MODEL),
        'w_up': nrm((DEPTH, D_MODEL, D_FF), D_MODEL ** -0.5),
        'w_down': nrm((DEPTH, D_FF, D_MODEL), D_FF ** -0.5),
        'g_pre_ple': gain(D_MODEL),
        'g_post_ple': gain(D_MODEL),
        'w_ple': nrm((DEPTH, PLE_DIM, D_MODEL), PLE_DIM ** -0.5),
        'w_ple_gate': nrm((DEPTH, D_MODEL, D_MODEL), D_MODEL ** -0.5),
    }


def reference(x_prompt, x_sample, p_prompt, p_sample, cache_sb_k, cache_sb_v, state_conv,
              state_mlstm_C, state_mlstm_n, state_mlstm_m, w_in, b_if, conv_w, conv_b, ml_norm,
              w_br_a, w_br_b, w_out, g_pre_mix, g_post_mix, g_pre_mlp, g_post_mlp, w_up, w_down,
              g_pre_ple, g_post_ple, w_ple, w_ple_gate):
    bp = x_prompt.shape[0]
    f32 = jnp.float32
    yp, ys = x_prompt, x_sample
    st_p, st_s = [], []
    for i in range(DEPTH):
        lw = (w_in[i], b_if[i], conv_w[i], conv_b[i], ml_norm[i], w_br_a[i], w_br_b[i], w_out[i],
              g_pre_mix[i], g_post_mix[i], g_pre_mlp[i], g_post_mlp[i], w_up[i], w_down[i],
              g_pre_ple[i], g_post_ple[i], w_ple[i], w_ple_gate[i])
        empty_kv = jnp.zeros((bp, 0, SB_HEADS, SB_HEAD_DIM), x_prompt.dtype)
        yp, sp = hybrid_layer(yp, p_prompt[i],
                              jnp.zeros((bp, CONV_W - 1, 2 * MIX_A), x_prompt.dtype),
                              jnp.zeros((bp, ML_HEADS, ML_HEAD_DIM, ML_HEAD_DIM), f32),
                              jnp.zeros((bp, ML_HEADS, ML_HEAD_DIM), f32),
                              jnp.zeros((bp, ML_HEADS), f32),
                              empty_kv, empty_kv, *lw)
        ys, ss = hybrid_layer(ys, p_sample[i], state_conv[i], state_mlstm_C[i], state_mlstm_n[i],
                              state_mlstm_m[i], cache_sb_k[i], cache_sb_v[i], *lw)
        st_p.append(sp)
        st_s.append(ss)
    stk = lambda sts, j: jnp.stack([s[j] for s in sts])
    return (yp, ys,
            stk(st_p, 0), stk(st_p, 1), stk(st_p, 2), stk(st_p, 3), stk(st_p, 4), stk(st_p, 5),
            stk(st_s, 0), stk(st_s, 1), stk(st_s, 2), stk(st_s, 3), stk(st_s, 4), stk(st_s, 5))
```

```python
import functools

import jax
import jax.numpy as jnp
from jax import lax
from jax.experimental import pallas as pl
from jax.experimental.pallas import tpu as pltpu

F32 = jnp.float32
BF16 = jnp.bfloat16
EPS = 1e-6

LANES = 128
SUBLANES = 8
VMEM_LIMIT = 56 * 1024 * 1024

ML_HEADS = 4
SB_HEADS = 8
CONV_W = 4
ML_CHUNK = 256
SB_TILE = 256
GATE_W = 2 * LANES

NT_DIMS = (((1,), (1,)), ((), ()))


def _rms(x, g):
    return x * lax.rsqrt(jnp.mean(x * x, axis=-1, keepdims=True) + EPS) * g


def _sigmoid(x):
    return 1.0 / (1.0 + jnp.exp(-x))


def _softplus(z):
    return jnp.maximum(z, 0.0) + jnp.log(1.0 + jnp.exp(-jnp.abs(z)))


def _pad_rows(x, rows):
    if x.shape[0] == rows:
        return x
    return jnp.concatenate([x, jnp.zeros((rows - x.shape[0],) + x.shape[1:], x.dtype)], axis=0)


def _dot(a, b):
    return jnp.dot(a, b, preferred_element_type=F32)


def _params(sem):
    return pltpu.CompilerParams(dimension_semantics=sem, vmem_limit_bytes=VMEM_LIMIT)


def _resident(shape):
    return pl.BlockSpec(shape, lambda *_: (0,) * len(shape), pipeline_mode=pl.Buffered(1))


def _in_proj_kernel(x_ref, g_ref, w_ref, wif_ref, p_ref, if_ref, h_scr):
    @pl.when(pl.program_id(1) == 0)
    def _():
        h = _rms(x_ref[...], g_ref[...]).astype(BF16)
        h_scr[...] = h
        if_ref[...] = _dot(h, wif_ref[...])

    p_ref[...] = _dot(h_scr[...], w_ref[...])


def _in_proj(x2, g, w_main, w_if, tm=512, tn=1024):
    m, d = x2.shape
    n = w_main.shape[1]
    return pl.pallas_call(
        _in_proj_kernel,
        grid=(m // tm, n // tn),
        in_specs=[pl.BlockSpec((tm, d), lambda i, j: (i, 0)),
                  pl.BlockSpec((1, d), lambda i, j: (0, 0)),
                  pl.BlockSpec((d, tn), lambda i, j: (0, j)),
                  pl.BlockSpec((d, GATE_W), lambda i, j: (0, 0))],
        out_specs=[pl.BlockSpec((tm, tn), lambda i, j: (i, j)),
                   pl.BlockSpec((tm, GATE_W), lambda i, j: (i, 0))],
        out_shape=[jax.ShapeDtypeStruct((m, n), F32),
                   jax.ShapeDtypeStruct((m, GATE_W), F32)],
        scratch_shapes=[pltpu.VMEM((tm, d), BF16)],
        compiler_params=_params(("parallel", "arbitrary")),
        name="in_proj",
    )(x2, g, w_main, w_if)


def _split3(x):
    hi = x.astype(BF16)
    r = x - hi.astype(F32)
    mid = r.astype(BF16)
    lo = (r - mid.astype(F32)).astype(BF16)
    return hi, mid, lo


def _mlstm_kernel(qk_ref, v_ref, o_ref, if_ref, cst_ref, c0_ref, n0_ref, m0_ref,
                  cw_ref, cb_ref, bif_ref, mln_ref, tri_ref,
                  h_ref, cout_ref, nout_ref, mout_ref, cnew_ref,
                  ext_scr, c_scr, n_scr, m_scr, *, L, Lp, H, dh):
    c = pl.program_id(1)
    mix = H * dh

    @pl.when(c == 0)
    def _():
        ext_scr[0:SUBLANES, :] = cst_ref[0]
        c_scr[...] = c0_ref[0]
        n_scr[...] = n0_ref[0]
        m_scr[...] = m0_ref[0]

    u = qk_ref[0]
    ext_scr[SUBLANES:SUBLANES + L, :] = u
    conv = cb_ref[...] + cw_ref[CONV_W - 1:CONV_W, :] * u
    for j in range(CONV_W - 1):
        lo = SUBLANES - (CONV_W - 1) + j
        conv = conv + cw_ref[j:j + 1, :] * ext_scr[lo:lo + L, :]
    tail = ext_scr[L:L + SUBLANES, :]
    ext_scr[0:SUBLANES, :] = tail
    cnew_ref[0] = tail
    qk = conv * _sigmoid(conv)
    q_all = qk[:, :mix]
    k_all = qk[:, mix:] * (dh ** -0.5)

    ifp = if_ref[0] + bif_ref[...]
    ig = ifp[:, 0:LANES]
    lf = -_softplus(-ifp[:, LANES:2 * LANES])
    tri = tri_ref[...]
    hi, mid, lo3 = _split3(_pad_rows(lf, Lp))
    b = _dot(tri, hi) + _dot(tri, mid) + _dot(tri, lo3)
    g = ig - b
    g_t = _pad_rows(g, Lp).T
    m_old = m_scr[...]
    m_last = jnp.maximum(m_old, jnp.max(g, axis=0, keepdims=True))
    cdec = jnp.exp(m_old - m_last)
    m_scr[...] = b[L - 1:L, :] + m_last
    wk = jnp.exp(g - m_last)

    row = lax.broadcasted_iota(jnp.int32, (L, Lp), 0)
    col = lax.broadcasted_iota(jnp.int32, (L, Lp), 1)
    causal = col <= row

    for h in range(H):
        sl = slice(h * dh, (h + 1) * dh)
        qh = q_all[:, sl]
        kh = k_all[:, sl]
        qb = qh.astype(BF16)
        kb = _pad_rows(kh, Lp).astype(BF16)
        vb = _pad_rows(v_ref[0, :, sl], Lp).astype(BF16)
        m_h = m_old[:, h:h + 1]
        dm = jnp.where(causal, g_t[h:h + 1, :], -jnp.inf)
        m_col = jnp.maximum(jnp.max(dm, axis=1, keepdims=True), m_h)
        s = lax.dot_general(qb, kb, NT_DIMS, preferred_element_type=F32) * jnp.exp(dm - m_col)
        dec = jnp.exp(m_h - m_col)
        num = _dot(s.astype(BF16), vb) + dec * _dot(qb, c_scr[h].astype(BF16))
        qn = jnp.sum(qh * n_scr[h:h + 1, :], axis=1, keepdims=True)
        den = jnp.sum(s, axis=1, keepdims=True) + dec * qn
        m_t = b[:, h:h + 1] + m_col
        hh = num / jnp.maximum(jnp.abs(den), jnp.exp(-m_t))
        hm = _sigmoid(o_ref[0, :, sl]) * hh
        h_ref[0, :, sl] = _rms(hm, mln_ref[:, sl]).astype(h_ref.dtype)
        kw = kh * wk[:, h:h + 1]
        cd_h = cdec[:, h:h + 1]
        c_scr[h] = cd_h * c_scr[h] + _dot(_pad_rows(kw, Lp).T.astype(BF16), vb)
        n_scr[h:h + 1, :] = cd_h * n_scr[h:h + 1, :] + jnp.sum(kw, axis=0, keepdims=True)

    @pl.when(c == pl.num_programs(1) - 1)
    def _():
        cout_ref[0] = c_scr[...]
        nout_ref[0] = n_scr[...]
        mout_ref[0] = m_scr[...]


def _mlstm(p3, if3, cst8, c0, n0, m0p, cw, cb, bif, mln, L):
    bsz, t, _ = p3.shape
    H = ML_HEADS
    mix = mln.shape[1]
    dh = mix // H
    nc = t // L
    Lp = max(L, LANES)
    tri = (jnp.arange(Lp)[None, :] <= jnp.arange(L)[:, None]).astype(BF16)
    kern = functools.partial(_mlstm_kernel, L=L, Lp=Lp, H=H, dh=dh)
    return pl.pallas_call(
        kern,
        grid=(bsz, nc),
        in_specs=[pl.BlockSpec((1, L, 2 * mix), lambda b, c: (b, c, 0)),
                  pl.BlockSpec((1, L, mix), lambda b, c: (b, c, 6)),
                  pl.BlockSpec((1, L, mix), lambda b, c: (b, c, 7)),
                  pl.BlockSpec((1, L, GATE_W), lambda b, c: (b, c, 0)),
                  pl.BlockSpec((1, SUBLANES, 2 * mix), lambda b, c: (b, 0, 0)),
                  pl.BlockSpec((1, H, dh, dh), lambda b, c: (b, 0, 0, 0)),
                  pl.BlockSpec((1, H, dh), lambda b, c: (b, 0, 0)),
                  pl.BlockSpec((1, 1, LANES), lambda b, c: (b, 0, 0)),
                  pl.BlockSpec((CONV_W, 2 * mix), lambda b, c: (0, 0)),
                  pl.BlockSpec((1, 2 * mix), lambda b, c: (0, 0)),
                  pl.BlockSpec((1, GATE_W), lambda b, c: (0, 0)),
                  pl.BlockSpec((1, mix), lambda b, c: (0, 0)),
                  pl.BlockSpec((L, Lp), lambda b, c: (0, 0))],
        out_specs=[pl.BlockSpec((1, L, mix), lambda b, c: (b, c, 0)),
                   pl.BlockSpec((1, H, dh, dh), lambda b, c: (b, 0, 0, 0)),
                   pl.BlockSpec((1, H, dh), lambda b, c: (b, 0, 0)),
                   pl.BlockSpec((1, 1, LANES), lambda b, c: (b, 0, 0)),
                   pl.BlockSpec((1, SUBLANES, 2 * mix), lambda b, c: (b, 0, 0))],
        out_shape=[jax.ShapeDtypeStruct((bsz, t, mix), BF16),
                   jax.ShapeDtypeStruct((bsz, H, dh, dh), F32),
                   jax.ShapeDtypeStruct((bsz, H, dh), F32),
                   jax.ShapeDtypeStruct((bsz, 1, LANES), F32),
                   jax.ShapeDtypeStruct((bsz, SUBLANES, 2 * mix), F32)],
        scratch_shapes=[pltpu.VMEM((L + SUBLANES, 2 * mix), F32),
                        pltpu.VMEM((H, dh, dh), F32),
                        pltpu.VMEM((H, dh), F32),
                        pltpu.VMEM((1, LANES), F32)],
        compiler_params=_params(("parallel", "arbitrary")),
        name="mlstm",
    )(p3, p3, p3, if3, cst8, c0, n0, m0p, cw, cb, bif, mln, tri)


def _sb_block(q, kblk, vblk, u_mat, carry, scale, mask):
    z = lax.dot_general(q, kblk, NT_DIMS, preferred_element_type=F32) * scale
    lk = -_softplus(z)
    if mask is not None:
        lk = jnp.where(mask, lk, 0.0)
    hi = lk.astype(BF16)
    lo = (lk - hi.astype(F32)).astype(BF16)
    r = _dot(hi, u_mat) + _dot(lo, u_mat)
    a = jnp.exp(z + r + carry)
    if mask is not None:
        a = jnp.where(mask, a, 0.0)
    return _dot(a.astype(BF16), vblk), carry + r[:, 0:1]


def _strict_causal(tq, sk):
    row = lax.broadcasted_iota(jnp.int32, (tq, sk), 0)
    col = lax.broadcasted_iota(jnp.int32, (tq, sk), 1)
    return col < row


def _sb_prompt_kernel(q_ref, k_ref, v_ref, u_ref, o_ref, kb_scr, vb_scr, acc_scr, car_scr,
                      *, tq, scale):
    qi = pl.program_id(2)

    @pl.when(qi == 0)
    def _():
        kb_scr[...] = k_ref[0].astype(BF16)
        vb_scr[...] = v_ref[0].astype(BF16)

    q = q_ref[0].astype(BF16)
    u_mat = u_ref[...]
    off = pl.multiple_of(qi * tq, tq)
    pv, car = _sb_block(q, kb_scr[pl.ds(off, tq), :], vb_scr[pl.ds(off, tq), :], u_mat,
                        jnp.zeros((tq, 1), F32), scale, _strict_causal(tq, tq))
    acc_scr[...] = pv
    car_scr[...] = car

    def body(i, carry):
        o2 = pl.multiple_of((qi - 1 - i) * tq, tq)
        pv2, car2 = _sb_block(q, kb_scr[pl.ds(o2, tq), :], vb_scr[pl.ds(o2, tq), :], u_mat,
                              car_scr[...], scale, None)
        acc_scr[...] += pv2
        car_scr[...] = car2
        return carry

    lax.fori_loop(0, qi, body, 0)
    o_ref[0] = acc_scr[...].astype(o_ref.dtype)


def _sb_prompt(p3, d_head, tq=SB_TILE):
    bsz, t, _ = p3.shape
    H = SB_HEADS
    qb, kb, vb = 8 * H, 9 * H, 10 * H
    u_mat = (jnp.arange(tq)[:, None] >= jnp.arange(tq)[None, :]).astype(BF16)
    kern = functools.partial(_sb_prompt_kernel, tq=tq, scale=d_head ** -0.5)
    return pl.pallas_call(
        kern,
        grid=(bsz, H, t // tq),
        in_specs=[pl.BlockSpec((1, tq, d_head), lambda b, h, i: (b, i, qb + h)),
                  pl.BlockSpec((1, t, d_head), lambda b, h, i: (b, 0, kb + h)),
                  pl.BlockSpec((1, t, d_head), lambda b, h, i: (b, 0, vb + h)),
                  pl.BlockSpec((tq, tq), lambda b, h, i: (0, 0))],
        out_specs=pl.BlockSpec((1, tq, d_head), lambda b, h, i: (b, i, h)),
        out_shape=jax.ShapeDtypeStruct((bsz, t, H * d_head), BF16),
        scratch_shapes=[pltpu.VMEM((t, d_head), BF16),
                        pltpu.VMEM((t, d_head), BF16),
                        pltpu.VMEM((tq, d_head), F32),
                        pltpu.VMEM((tq, 1), F32)],
        compiler_params=_params(("parallel", "parallel", "arbitrary")),
        name="sb_prompt",
    )(p3, p3, p3, u_mat)


def _sb_sample_kernel(q_ref, kn_ref, vn_ref, ck_ref, cv_ref, un_ref, up_ref, o_ref,
                      *, L, Lp, sk, nblk, scale):
    q = q_ref[0].astype(BF16)
    kn = _pad_rows(kn_ref[0], Lp).astype(BF16)
    vn = _pad_rows(vn_ref[0], Lp).astype(BF16)
    pv, car = _sb_block(q, kn, vn, un_ref[...], jnp.zeros((L, 1), F32), scale,
                        _strict_causal(L, Lp))
    u_mat = up_ref[...]

    def body(i, carry):
        acc, car_i = carry
        off = pl.multiple_of((nblk - 1 - i) * sk, sk)
        kblk = ck_ref[0, pl.ds(off, sk), :].astype(BF16)
        vblk = cv_ref[0, pl.ds(off, sk), :].astype(BF16)
        pv2, car2 = _sb_block(q, kblk, vblk, u_mat, car_i, scale, None)
        return acc + pv2, car2

    acc, _ = lax.fori_loop(0, nblk, body, (pv, car))
    o_ref[0] = acc.astype(o_ref.dtype)


def _sb_sample(p3, ck, cv, d_head, sk=SB_TILE):
    bsz, L, _ = p3.shape
    past = ck.shape[1]
    assert past % sk == 0
    H = SB_HEADS
    Lp = max(L, LANES)
    qb, kb, vb = 8 * H, 9 * H, 10 * H
    u_new = (jnp.arange(Lp)[:, None] >= jnp.arange(Lp)[None, :]).astype(BF16)
    u_past = (jnp.arange(sk)[:, None] >= jnp.arange(sk)[None, :]).astype(BF16)
    kern = functools.partial(_sb_sample_kernel, L=L, Lp=Lp, sk=sk, nblk=past // sk,
                             scale=d_head ** -0.5)
    return pl.pallas_call(
        kern,
        grid=(bsz, H),
        in_specs=[pl.BlockSpec((1, L, d_head), lambda b, h: (b, 0, qb + h)),
                  pl.BlockSpec((1, L, d_head), lambda b, h: (b, 0, kb + h)),
                  pl.BlockSpec((1, L, d_head), lambda b, h: (b, 0, vb + h)),
                  pl.BlockSpec((1, past, d_head), lambda b, h: (b, 0, h)),
                  pl.BlockSpec((1, past, d_head), lambda b, h: (b, 0, h)),
                  pl.BlockSpec((Lp, Lp), lambda b, h: (0, 0)),
                  pl.BlockSpec((sk, sk), lambda b, h: (0, 0))],
        out_specs=pl.BlockSpec((1, L, d_head), lambda b, h: (b, 0, h)),
        out_shape=jax.ShapeDtypeStruct((bsz, L, H * d_head), BF16),
        compiler_params=_params(("parallel", "parallel")),
        name="sb_sample",
    )(p3, p3, p3, ck, cv, u_new, u_past)


def _mix_out_kernel(x_ref, hm_ref, hs_ref, ga_ref, gb_ref, wa_ref, wb_ref, wo_ref, g_ref, o_ref):
    ua = _dot(hm_ref[...], wa_ref[...])
    ub = _dot(hs_ref[...], wb_ref[...])
    u = _sigmoid(ga_ref[...]) * ua + _sigmoid(gb_ref[...]) * ub
    y = _dot(u.astype(BF16), wo_ref[...])
    o_ref[...] = x_ref[...] + _rms(y, g_ref[...])


def _mix_out(x2, hm, hs, p2, wa, wb, wo, g, tm=256):
    m, d = x2.shape
    mix = hm.shape[1]
    return pl.pallas_call(
        _mix_out_kernel,
        grid=(m // tm,),
        in_specs=[pl.BlockSpec((tm, d), lambda i: (i, 0)),
                  pl.BlockSpec((tm, mix), lambda i: (i, 0)),
                  pl.BlockSpec((tm, mix), lambda i: (i, 0)),
                  pl.BlockSpec((tm, d), lambda i: (i, 1)),
                  pl.BlockSpec((tm, d), lambda i: (i, 2)),
                  _resident((mix, d)), _resident((mix, d)), _resident((d, d)),
                  _resident((1, d))],
        out_specs=pl.BlockSpec((tm, d), lambda i: (i, 0)),
        out_shape=jax.ShapeDtypeStruct((m, d), F32),
        compiler_params=_params(("parallel",)),
        name="mix_out",
    )(x2, hm, hs, p2, p2, wa, wb, wo, g)


def _mlp_kernel(x_ref, g1_ref, wu_ref, wd_ref, g2_ref, o_ref, h_scr, acc_scr):
    j = pl.program_id(1)

    @pl.when(j == 0)
    def _():
        h_scr[...] = _rms(x_ref[...], g1_ref[...]).astype(BF16)

    a = _dot(h_scr[...], wu_ref[...])
    a = jnp.square(jnp.maximum(a, 0.0)).astype(BF16)
    part = _dot(a, wd_ref[...])

    @pl.when(j == 0)
    def _():
        acc_scr[...] = part

    @pl.when(j > 0)
    def _():
        acc_scr[...] += part

    @pl.when(j == pl.num_programs(1) - 1)
    def _():
        o_ref[...] = x_ref[...] + _rms(acc_scr[...], g2_ref[...])


def _mlp(x2, g1, wu, wd, g2, tm=512, tf=1024):
    m, d = x2.shape
    dff = wu.shape[1]
    return pl.pallas_call(
        _mlp_kernel,
        grid=(m // tm, dff // tf),
        in_specs=[pl.BlockSpec((tm, d), lambda i, j: (i, 0)),
                  pl.BlockSpec((1, d), lambda i, j: (0, 0)),
                  pl.BlockSpec((d, tf), lambda i, j: (0, j)),
                  pl.BlockSpec((tf, d), lambda i, j: (j, 0)),
                  pl.BlockSpec((1, d), lambda i, j: (0, 0))],
        out_specs=pl.BlockSpec((tm, d), lambda i, j: (i, 0)),
        out_shape=jax.ShapeDtypeStruct((m, d), F32),
        scratch_shapes=[pltpu.VMEM((tm, d), BF16), pltpu.VMEM((tm, d), F32)],
        compiler_params=_params(("parallel", "arbitrary")),
        name="mlp",
    )(x2, g1, wu, wd, g2)


def _ple_kernel(x_ref, p_ref, g1_ref, wg_ref, wp_ref, g2_ref, o_ref):
    x = x_ref[...]
    gate = _sigmoid(_dot(_rms(x, g1_ref[...]).astype(BF16), wg_ref[...]))
    ple = _dot(p_ref[...].astype(BF16), wp_ref[...]) * gate
    o_ref[...] = x + _rms(ple, g2_ref[...])


def _ple(x2, p2, g1, wg, wp, g2, tm=512):
    m, d = x2.shape
    pd = p2.shape[1]
    return pl.pallas_call(
        _ple_kernel,
        grid=(m // tm,),
        in_specs=[pl.BlockSpec((tm, d), lambda i: (i, 0)),
                  pl.BlockSpec((tm, pd), lambda i: (i, 0)),
                  _resident((1, d)), _resident((d, d)), _resident((pd, d)), _resident((1, d))],
        out_specs=pl.BlockSpec((tm, d), lambda i: (i, 0)),
        out_shape=jax.ShapeDtypeStruct((m, d), F32),
        compiler_params=_params(("parallel",)),
        name="ple",
    )(x2, p2, g1, wg, wp, g2)


def _prep_weights(w_in, b_if, conv_w, conv_b, ml_norm, w_br_a, w_br_b, w_out, g_pre_mix, g_post_mix,
                  g_pre_mlp, g_post_mlp, w_up, w_down, g_pre_ple, g_post_ple, w_ple, w_ple_gate):
    d = w_in.shape[0]
    mix = ml_norm.shape[0]
    H = ML_HEADS
    o_if = 4 * mix
    o_sb = o_if + 2 * H
    o_gate = o_sb + 3 * mix
    w_main = jnp.concatenate([w_in[:, :2 * mix], w_in[:, o_gate:], w_in[:, 2 * mix:o_if],
                              w_in[:, o_sb:o_gate]], axis=1).astype(BF16)
    zpad = jnp.zeros((d, LANES - H), w_in.dtype)
    w_if = jnp.concatenate([w_in[:, o_if:o_if + H], zpad, w_in[:, o_if + H:o_sb], zpad],
                           axis=1).astype(BF16)
    bpad = jnp.zeros((LANES - H,), F32)
    bif = jnp.concatenate([b_if[:H], bpad, b_if[H:], bpad])[None, :]
    row = lambda a: a[None, :].astype(F32)
    return dict(w_main=w_main, w_if=w_if, bif=bif, cw=conv_w.astype(F32), cb=row(conv_b),
                mln=row(ml_norm), wa=w_br_a.astype(BF16), wb=w_br_b.astype(BF16),
                wo=w_out.astype(BF16), g_pre_mix=row(g_pre_mix), g_post_mix=row(g_post_mix),
                g_pre_mlp=row(g_pre_mlp), g_post_mlp=row(g_post_mlp), wu=w_up.astype(BF16),
                wd=w_down.astype(BF16), g_pre_ple=row(g_pre_ple), g_post_ple=row(g_post_ple),
                wp=w_ple.astype(BF16), wg=w_ple_gate.astype(BF16))


def _layer(x, p, conv_buf, c0, n0, m0, kv_past, W):
    bsz, t, d = x.shape
    m_rows = bsz * t
    mix = W["mln"].shape[1]
    d_head = mix // SB_HEADS
    x2 = x.reshape(m_rows, d)
    proj, gates = _in_proj(x2, W["g_pre_mix"], W["w_main"], W["w_if"])
    p3 = proj.reshape(bsz, t, -1)
    if3 = gates.reshape(bsz, t, GATE_W)

    cst8 = jnp.pad(conv_buf.astype(F32), ((0, 0), (SUBLANES - (CONV_W - 1), 0), (0, 0)))
    m0p = jnp.pad(m0.astype(F32), ((0, 0), (0, LANES - ML_HEADS)))[:, None, :]
    L = min(t, ML_CHUNK)
    h_m, c_new, n_new, m_new, conv8 = _mlstm(p3, if3, cst8, c0.astype(F32), n0.astype(F32), m0p,
                                             W["cw"], W["cb"], W["bif"], W["mln"], L)
    if kv_past is None:
        h_s = _sb_prompt(p3, d_head)
    else:
        k_past, v_past = kv_past
        past = k_past.shape[1]
        h_s = _sb_sample(p3, k_past.reshape(bsz, past, mix), v_past.reshape(bsz, past, mix), d_head)

    x1 = _mix_out(x2, h_m.reshape(m_rows, mix), h_s.reshape(m_rows, mix), proj,
                  W["wa"], W["wb"], W["wo"], W["g_post_mix"])
    x3 = _mlp(x1, W["g_pre_mlp"], W["wu"], W["wd"], W["g_post_mlp"])
    y = _ple(x3, p.reshape(m_rows, -1), W["g_pre_ple"], W["wg"], W["wp"], W["g_post_ple"])

    k_s = p3[:, :, 9 * mix:10 * mix].reshape(bsz, t, SB_HEADS, d_head)
    v_s = p3[:, :, 10 * mix:11 * mix].reshape(bsz, t, SB_HEADS, d_head)
    state = (k_s, v_s, conv8[:, SUBLANES - (CONV_W - 1):, :], c_new, n_new,
             m_new[:, 0, :ML_HEADS])
    return y.reshape(bsz, t, d), state


def kernel(x_prompt, x_sample, p_prompt, p_sample, cache_sb_k, cache_sb_v, state_conv,
           state_mlstm_C, state_mlstm_n, state_mlstm_m, w_in, b_if, conv_w, conv_b, ml_norm,
           w_br_a, w_br_b, w_out, g_pre_mix, g_post_mix, g_pre_mlp, g_post_mlp, w_up, w_down,
           g_pre_ple, g_post_ple, w_ple, w_ple_gate):
    depth = w_in.shape[0]
    bp = x_prompt.shape[0]
    mix = ml_norm.shape[1]
    dh = mix // ML_HEADS
    yp, ys = x_prompt, x_sample
    st_p, st_s = [], []
    for i in range(depth):
        W = _prep_weights(w_in[i], b_if[i], conv_w[i], conv_b[i], ml_norm[i], w_br_a[i],
                          w_br_b[i], w_out[i], g_pre_mix[i], g_post_mix[i], g_pre_mlp[i],
                          g_post_mlp[i], w_up[i], w_down[i], g_pre_ple[i], g_post_ple[i],
                          w_ple[i], w_ple_gate[i])
        yp, sp = _layer(yp, p_prompt[i],
                        jnp.zeros((bp, CONV_W - 1, 2 * mix), F32),
                        jnp.zeros((bp, ML_HEADS, dh, dh), F32),
                        jnp.zeros((bp, ML_HEADS, dh), F32),
                        jnp.zeros((bp, ML_HEADS), F32),
                        None, W)
        ys, ss = _layer(ys, p_sample[i], state_conv[i], state_mlstm_C[i], state_mlstm_n[i],
                        state_mlstm_m[i], (cache_sb_k[i], cache_sb_v[i]), W)
        st_p.append(sp)
        st_s.append(ss)
    stk = lambda sts, j: jnp.stack([s[j] for s in sts])
    return (yp, ys,
            stk(st_p, 0), stk(st_p, 1), stk(st_p, 2), stk(st_p, 3), stk(st_p, 4), stk(st_p, 5),
            stk(st_s, 0), stk(st_s, 1), stk(st_s, 2), stk(st_s, 3), stk(st_s, 4), stk(st_s, 5))
```

```python
import functools

import jax
import jax.numpy as jnp
from jax import lax
from jax.experimental import pallas as pl
from jax.experimental.pallas import tpu as pltpu

F32 = jnp.float32
BF16 = jnp.bfloat16
EPS = 1e-6

LANES = 128
SUBLANES = 8
VMEM_LIMIT = 56 * 1024 * 1024

ML_HEADS = 4
SB_HEADS = 8
CONV_W = 4
ML_CHUNK = 256
SB_TILE = 256
GATE_W = 2 * LANES

NT_DIMS = (((1,), (1,)), ((), ()))


def _rms(x, g):
    return x * lax.rsqrt(jnp.mean(x * x, axis=-1, keepdims=True) + EPS) * g


def _sigmoid(x):
    return 1.0 / (1.0 + jnp.exp(-x))


def _softplus(z):
    return jnp.maximum(z, 0.0) + jnp.log(1.0 + jnp.exp(-jnp.abs(z)))


def _pad_rows(x, rows):
    if x.shape[0] == rows:
        return x
    return jnp.concatenate([x, jnp.zeros((rows - x.shape[0],) + x.shape[1:], x.dtype)], axis=0)


def _dot(a, b):
    return jnp.dot(a, b, preferred_element_type=F32)


def _params(sem):
    return pltpu.CompilerParams(dimension_semantics=sem, vmem_limit_bytes=VMEM_LIMIT)


def _resident(shape):
    return pl.BlockSpec(shape, lambda *_: (0,) * len(shape), pipeline_mode=pl.Buffered(1))


def _in_proj_kernel(x_ref, g_ref, w_ref, wif_ref, p_ref, qkv_ref, ks_ref, vs_ref, if_ref, h_scr,
                    *, n_main):
    j = pl.program_id(1)

    @pl.when(j == 0)
    def _():
        h = _rms(x_ref[...], g_ref[...]).astype(BF16)
        h_scr[...] = h
        if_ref[...] = _dot(h, wif_ref[...])

    acc = _dot(h_scr[...], w_ref[...])

    @pl.when(j < n_main)
    def _():
        p_ref[...] = acc

    @pl.when(j >= n_main)
    def _():
        qkv_ref[...] = acc.astype(BF16)

    @pl.when(j == n_main + 1)
    def _():
        ks_ref[...] = acc

    @pl.when(j == n_main + 2)
    def _():
        vs_ref[...] = acc


def _in_proj(x2, g, w_main, w_if, mix, tm=512):
    m, d = x2.shape
    tn = mix
    assert m % tm == 0 and w_main.shape[1] % tn == 0
    n_tiles = w_main.shape[1] // tn
    n_main = n_tiles - 3
    kern = functools.partial(_in_proj_kernel, n_main=n_main)
    return pl.pallas_call(
        kern,
        grid=(m // tm, n_tiles),
        in_specs=[pl.BlockSpec((tm, d), lambda i, j: (i, 0)),
                  pl.BlockSpec((1, d), lambda i, j: (0, 0)),
                  pl.BlockSpec((d, tn), lambda i, j: (0, j)),
                  pl.BlockSpec((d, GATE_W), lambda i, j: (0, 0))],
        out_specs=[pl.BlockSpec((tm, tn), lambda i, j: (i, jnp.minimum(j, n_main - 1))),
                   pl.BlockSpec((tm, tn), lambda i, j: (i, jnp.maximum(j - n_main, 0))),
                   pl.BlockSpec((tm, tn), lambda i, j: (i, 0)),
                   pl.BlockSpec((tm, tn), lambda i, j: (i, 0)),
                   pl.BlockSpec((tm, GATE_W), lambda i, j: (i, 0))],
        out_shape=[jax.ShapeDtypeStruct((m, n_main * tn), F32),
                   jax.ShapeDtypeStruct((m, 3 * tn), BF16),
                   jax.ShapeDtypeStruct((m, tn), F32),
                   jax.ShapeDtypeStruct((m, tn), F32),
                   jax.ShapeDtypeStruct((m, GATE_W), F32)],
        scratch_shapes=[pltpu.VMEM((tm, d), BF16)],
        compiler_params=_params(("parallel", "arbitrary")),
        name="in_proj",
    )(x2, g, w_main, w_if)


def _split3(x):
    hi = x.astype(BF16)
    r = x - hi.astype(F32)
    mid = r.astype(BF16)
    lo = (r - mid.astype(F32)).astype(BF16)
    return hi, mid, lo


def _mlstm_kernel(qk_ref, v_ref, o_ref, if_ref, cst_ref, c0_ref, n0_ref, m0_ref,
                  cw_ref, cb_ref, bif_ref, mln_ref, tri_ref,
                  h_ref, cout_ref, nout_ref, mout_ref, cnew_ref,
                  ext_scr, c_scr, n_scr, m_scr, *, L, Lp, H, dh):
    c = pl.program_id(1)
    mix = H * dh

    @pl.when(c == 0)
    def _():
        ext_scr[0:SUBLANES, :] = cst_ref[0]
        c_scr[...] = c0_ref[0]
        n_scr[...] = n0_ref[0]
        m_scr[...] = m0_ref[0]

    u = qk_ref[0]
    ext_scr[SUBLANES:SUBLANES + L, :] = u
    conv = cb_ref[...] + cw_ref[CONV_W - 1:CONV_W, :] * u
    for j in range(CONV_W - 1):
        lo = SUBLANES - (CONV_W - 1) + j
        conv = conv + cw_ref[j:j + 1, :] * ext_scr[lo:lo + L, :]
    tail = ext_scr[L:L + SUBLANES, :]
    ext_scr[0:SUBLANES, :] = tail
    cnew_ref[0] = tail
    qk = conv * _sigmoid(conv)
    q_all = qk[:, :mix]
    k_all = qk[:, mix:] * (dh ** -0.5)

    ifp = if_ref[0] + bif_ref[...]
    ig = ifp[:, 0:LANES]
    lf = -_softplus(-ifp[:, LANES:2 * LANES])
    tri = tri_ref[...]
    hi, mid, lo3 = _split3(_pad_rows(lf, Lp))
    b = _dot(tri, hi) + _dot(tri, mid) + _dot(tri, lo3)
    g = ig - b
    g_t = _pad_rows(g, Lp).T
    m_old = m_scr[...]
    m_last = jnp.maximum(m_old, jnp.max(g, axis=0, keepdims=True))
    cdec = jnp.exp(m_old - m_last)
    m_scr[...] = b[L - 1:L, :] + m_last
    wk = jnp.exp(g - m_last)

    row = lax.broadcasted_iota(jnp.int32, (L, Lp), 0)
    col = lax.broadcasted_iota(jnp.int32, (L, Lp), 1)
    causal = col <= row

    for h in range(H):
        sl = slice(h * dh, (h + 1) * dh)
        qh = q_all[:, sl]
        kh = k_all[:, sl]
        qb = qh.astype(BF16)
        kb = _pad_rows(kh, Lp).astype(BF16)
        vb = _pad_rows(v_ref[0, :, sl], Lp).astype(BF16)
        m_h = m_old[:, h:h + 1]
        dm = jnp.where(causal, g_t[h:h + 1, :], -jnp.inf)
        m_col = jnp.maximum(jnp.max(dm, axis=1, keepdims=True), m_h)
        s = lax.dot_general(qb, kb, NT_DIMS, preferred_element_type=F32) * jnp.exp(dm - m_col)
        dec = jnp.exp(m_h - m_col)
        num = _dot(s.astype(BF16), vb) + dec * _dot(qb, c_scr[h].astype(BF16))
        qn = jnp.sum(qh * n_scr[h:h + 1, :], axis=1, keepdims=True)
        den = jnp.sum(s, axis=1, keepdims=True) + dec * qn
        m_t = b[:, h:h + 1] + m_col
        hh = num / jnp.maximum(jnp.abs(den), jnp.exp(-m_t))
        hm = _sigmoid(o_ref[0, :, sl]) * hh
        h_ref[0, :, sl] = _rms(hm, mln_ref[:, sl]).astype(h_ref.dtype)
        kw = kh * wk[:, h:h + 1]
        cd_h = cdec[:, h:h + 1]
        c_scr[h] = cd_h * c_scr[h] + _dot(_pad_rows(kw, Lp).T.astype(BF16), vb)
        n_scr[h:h + 1, :] = cd_h * n_scr[h:h + 1, :] + jnp.sum(kw, axis=0, keepdims=True)

    @pl.when(c == pl.num_programs(1) - 1)
    def _():
        cout_ref[0] = c_scr[...]
        nout_ref[0] = n_scr[...]
        mout_ref[0] = m_scr[...]


def _mlstm(p3, if3, cst8, c0, n0, m0p, cw, cb, bif, mln, L):
    bsz, t, _ = p3.shape
    H = ML_HEADS
    mix = mln.shape[1]
    dh = mix // H
    assert t % L == 0 and L % SUBLANES == 0
    nc = t // L
    Lp = max(L, LANES)
    tri = (jnp.arange(Lp)[None, :] <= jnp.arange(L)[:, None]).astype(BF16)
    kern = functools.partial(_mlstm_kernel, L=L, Lp=Lp, H=H, dh=dh)
    return pl.pallas_call(
        kern,
        grid=(bsz, nc),
        in_specs=[pl.BlockSpec((1, L, 2 * mix), lambda b, c: (b, c, 0)),
                  pl.BlockSpec((1, L, mix), lambda b, c: (b, c, 6)),
                  pl.BlockSpec((1, L, mix), lambda b, c: (b, c, 7)),
                  pl.BlockSpec((1, L, GATE_W), lambda b, c: (b, c, 0)),
                  pl.BlockSpec((1, SUBLANES, 2 * mix), lambda b, c: (b, 0, 0)),
                  pl.BlockSpec((1, H, dh, dh), lambda b, c: (b, 0, 0, 0)),
                  pl.BlockSpec((1, H, dh), lambda b, c: (b, 0, 0)),
                  pl.BlockSpec((1, 1, LANES), lambda b, c: (b, 0, 0)),
                  pl.BlockSpec((CONV_W, 2 * mix), lambda b, c: (0, 0)),
                  pl.BlockSpec((1, 2 * mix), lambda b, c: (0, 0)),
                  pl.BlockSpec((1, GATE_W), lambda b, c: (0, 0)),
                  pl.BlockSpec((1, mix), lambda b, c: (0, 0)),
                  pl.BlockSpec((L, Lp), lambda b, c: (0, 0))],
        out_specs=[pl.BlockSpec((1, L, mix), lambda b, c: (b, c, 0)),
                   pl.BlockSpec((1, H, dh, dh), lambda b, c: (b, 0, 0, 0)),
                   pl.BlockSpec((1, H, dh), lambda b, c: (b, 0, 0)),
                   pl.BlockSpec((1, 1, LANES), lambda b, c: (b, 0, 0)),
                   pl.BlockSpec((1, SUBLANES, 2 * mix), lambda b, c: (b, 0, 0))],
        out_shape=[jax.ShapeDtypeStruct((bsz, t, mix), BF16),
                   jax.ShapeDtypeStruct((bsz, H, dh, dh), F32),
                   jax.ShapeDtypeStruct((bsz, H, dh), F32),
                   jax.ShapeDtypeStruct((bsz, 1, LANES), F32),
                   jax.ShapeDtypeStruct((bsz, SUBLANES, 2 * mix), F32)],
        scratch_shapes=[pltpu.VMEM((L + SUBLANES, 2 * mix), F32),
                        pltpu.VMEM((H, dh, dh), F32),
                        pltpu.VMEM((H, dh), F32),
                        pltpu.VMEM((1, LANES), F32)],
        compiler_params=_params(("parallel", "arbitrary")),
        name="mlstm",
    )(p3, p3, p3, if3, cst8, c0, n0, m0p, cw, cb, bif, mln, tri)


def _sb_blocks(qs, kblks, vblks, u_mat, carries, scale, mask, group):
    n = len(qs)
    zs, rs, out = [None] * n, [None] * n, [None] * n

    def logits(h):
        zs[h] = lax.dot_general(qs[h], kblks[h], NT_DIMS, preferred_element_type=F32) * scale

    def cumsum(h):
        lk = -_softplus(zs[h])
        if mask is not None:
            lk = jnp.where(mask, lk, 0.0)
        hi = lk.astype(BF16)
        lo = (lk - hi.astype(F32)).astype(BF16)
        rs[h] = _dot(hi, u_mat) + _dot(lo, u_mat)

    def weights(h):
        a = jnp.exp(zs[h] + rs[h] + carries[h])
        if mask is not None:
            a = jnp.where(mask, a, 0.0)
        out[h] = (_dot(a.astype(BF16), vblks[h]), carries[h] + rs[h][:, 0:1])

    stages = (logits, cumsum, weights)
    groups = [range(g, min(g + group, n)) for g in range(0, n, group)]
    for step in range(len(groups) + len(stages) - 1):
        for k, stage in enumerate(stages):
            if 0 <= step - k < len(groups):
                for h in groups[step - k]:
                    stage(h)
    return out


def _strict_causal(tq, sk):
    row = lax.broadcasted_iota(jnp.int32, (tq, sk), 0)
    col = lax.broadcasted_iota(jnp.int32, (tq, sk), 1)
    return col < row


def _sb_prompt_kernel(q_ref, k_ref, v_ref, u_ref, o_ref, acc_scr, car_scr,
                      *, tq, dh, hpb, group, scale):
    qi = pl.program_id(2)
    u_mat = u_ref[...]
    heads = [slice(h * dh, (h + 1) * dh) for h in range(hpb)]
    qs = [q_ref[0, :, sl] for sl in heads]
    off = pl.multiple_of(qi * tq, tq)
    mask = _strict_causal(tq, tq)
    res = _sb_blocks(qs, [k_ref[0, pl.ds(off, tq), sl] for sl in heads],
                     [v_ref[0, pl.ds(off, tq), sl] for sl in heads], u_mat,
                     [jnp.zeros((tq, 1), F32)] * hpb, scale, mask, group)
    for h, (pv, car) in enumerate(res):
        acc_scr[h] = pv
        car_scr[h] = car

    def body(i, carry):
        o2 = pl.multiple_of((qi - 1 - i) * tq, tq)
        res2 = _sb_blocks(qs, [k_ref[0, pl.ds(o2, tq), sl] for sl in heads],
                          [v_ref[0, pl.ds(o2, tq), sl] for sl in heads], u_mat,
                          [car_scr[h] for h in range(hpb)], scale, None, group)
        for h, (pv2, car2) in enumerate(res2):
            acc_scr[h] += pv2
            car_scr[h] = car2
        return carry

    lax.fori_loop(0, qi, body, 0)
    for h, sl in enumerate(heads):
        o_ref[0, :, sl] = acc_scr[h].astype(o_ref.dtype)


def _sb_prompt(qkv3, d_head, tq=SB_TILE, hpb=4, group=1):
    bsz, t, _ = qkv3.shape
    H = SB_HEADS
    assert t % tq == 0 and H % hpb == 0
    ng = H // hpb
    w = hpb * d_head
    u_mat = (jnp.arange(tq)[:, None] >= jnp.arange(tq)[None, :]).astype(BF16)
    kern = functools.partial(_sb_prompt_kernel, tq=tq, dh=d_head, hpb=hpb, group=group,
                             scale=d_head ** -0.5)
    return pl.pallas_call(
        kern,
        grid=(bsz, ng, t // tq),
        in_specs=[pl.BlockSpec((1, tq, w), lambda b, g, i: (b, i, g)),
                  pl.BlockSpec((1, t, w), lambda b, g, i: (b, 0, ng + g)),
                  pl.BlockSpec((1, t, w), lambda b, g, i: (b, 0, 2 * ng + g)),
                  pl.BlockSpec((tq, tq), lambda b, g, i: (0, 0))],
        out_specs=pl.BlockSpec((1, tq, w), lambda b, g, i: (b, i, g)),
        out_shape=jax.ShapeDtypeStruct((bsz, t, H * d_head), BF16),
        scratch_shapes=[pltpu.VMEM((hpb, tq, d_head), F32),
                        pltpu.VMEM((hpb, tq, 1), F32)],
        compiler_params=_params(("parallel", "parallel", "arbitrary")),
        name="sb_prompt",
    )(qkv3, qkv3, qkv3, u_mat)


def _sb_sample_kernel(q_ref, kn_ref, vn_ref, ck_ref, cv_ref, un_ref, up_ref, o_ref, acc_scr, car_scr,
                      *, L, Lp, H, dh, sk, nsub, group, scale):
    c = pl.program_id(1)
    heads = [slice(h * dh, (h + 1) * dh) for h in range(H)]
    qs = [q_ref[0, :, sl] for sl in heads]

    @pl.when(c == 0)
    def _():
        mask = _strict_causal(L, Lp)
        res = _sb_blocks(qs, [_pad_rows(kn_ref[0, :, sl], Lp) for sl in heads],
                         [_pad_rows(vn_ref[0, :, sl], Lp) for sl in heads], un_ref[...],
                         [jnp.zeros((L, 1), F32)] * H, scale, mask, group)
        for h, (pv, car) in enumerate(res):
            acc_scr[h] = pv
            car_scr[h] = car

    u_mat = up_ref[...]

    def body(i, carry):
        off = pl.multiple_of((nsub - 1 - i) * sk, sk)
        res2 = _sb_blocks(qs, [ck_ref[0, pl.ds(off, sk), sl].astype(BF16) for sl in heads],
                          [cv_ref[0, pl.ds(off, sk), sl].astype(BF16) for sl in heads], u_mat,
                          [car_scr[h] for h in range(H)], scale, None, group)
        for h, (pv2, car2) in enumerate(res2):
            acc_scr[h] += pv2
            car_scr[h] = car2
        return carry

    lax.fori_loop(0, nsub, body, 0)

    @pl.when(c == pl.num_programs(1) - 1)
    def _():
        for h, sl in enumerate(heads):
            o_ref[0, :, sl] = acc_scr[h].astype(o_ref.dtype)


def _sb_sample(qkv3, ck, cv, d_head, sk=SB_TILE, chunk=1024, group=4):
    bsz, L, _ = qkv3.shape
    past = ck.shape[1]
    chunk = min(chunk, past)
    assert past % chunk == 0 and chunk % sk == 0
    H = SB_HEADS
    w = H * d_head
    nch = past // chunk
    Lp = max(L, LANES)
    u_new = (jnp.arange(Lp)[:, None] >= jnp.arange(Lp)[None, :]).astype(BF16)
    u_past = (jnp.arange(sk)[:, None] >= jnp.arange(sk)[None, :]).astype(BF16)
    kern = functools.partial(_sb_sample_kernel, L=L, Lp=Lp, H=H, dh=d_head, sk=sk,
                             nsub=chunk // sk, group=group, scale=d_head ** -0.5)
    return pl.pallas_call(
        kern,
        grid=(bsz, nch),
        in_specs=[pl.BlockSpec((1, L, w), lambda b, c: (b, 0, 0)),
                  pl.BlockSpec((1, L, w), lambda b, c: (b, 0, 1)),
                  pl.BlockSpec((1, L, w), lambda b, c: (b, 0, 2)),
                  pl.BlockSpec((1, chunk, w), lambda b, c: (b, nch - 1 - c, 0)),
                  pl.BlockSpec((1, chunk, w), lambda b, c: (b, nch - 1 - c, 0)),
                  pl.BlockSpec((Lp, Lp), lambda b, c: (0, 0)),
                  pl.BlockSpec((sk, sk), lambda b, c: (0, 0))],
        out_specs=pl.BlockSpec((1, L, w), lambda b, c: (b, 0, 0)),
        out_shape=jax.ShapeDtypeStruct((bsz, L, w), BF16),
        scratch_shapes=[pltpu.VMEM((H, L, d_head), F32),
                        pltpu.VMEM((H, L, 1), F32)],
        compiler_params=_params(("parallel", "arbitrary")),
        name="sb_sample",
    )(qkv3, qkv3, qkv3, ck, cv, u_new, u_past)


def _mix_out_kernel(x_ref, hm_ref, hs_ref, ga_ref, gb_ref, wa_ref, wb_ref, wo_ref, g_ref, o_ref):
    ua = _dot(hm_ref[...], wa_ref[...])
    ub = _dot(hs_ref[...], wb_ref[...])
    u = _sigmoid(ga_ref[...]) * ua + _sigmoid(gb_ref[...]) * ub
    y = _dot(u.astype(BF16), wo_ref[...])
    o_ref[...] = x_ref[...] + _rms(y, g_ref[...])


def _mix_out(x2, hm, hs, p2, wa, wb, wo, g, tm=256):
    m, d = x2.shape
    mix = hm.shape[1]
    assert m % tm == 0 and d == 2 * mix
    return pl.pallas_call(
        _mix_out_kernel,
        grid=(m // tm,),
        in_specs=[pl.BlockSpec((tm, d), lambda i: (i, 0)),
                  pl.BlockSpec((tm, mix), lambda i: (i, 0)),
                  pl.BlockSpec((tm, mix), lambda i: (i, 0)),
                  pl.BlockSpec((tm, d), lambda i: (i, 1)),
                  pl.BlockSpec((tm, d), lambda i: (i, 2)),
                  _resident((mix, d)), _resident((mix, d)), _resident((d, d)),
                  _resident((1, d))],
        out_specs=pl.BlockSpec((tm, d), lambda i: (i, 0)),
        out_shape=jax.ShapeDtypeStruct((m, d), F32),
        compiler_params=_params(("parallel",)),
        name="mix_out",
    )(x2, hm, hs, p2, p2, wa, wb, wo, g)


def _mlp_kernel(x_ref, g1_ref, wu_ref, wd_ref, g2_ref, o_ref, h_scr, acc_scr):
    j = pl.program_id(1)

    @pl.when(j == 0)
    def _():
        h_scr[...] = _rms(x_ref[...], g1_ref[...]).astype(BF16)

    a = _dot(h_scr[...], wu_ref[...])
    a = jnp.square(jnp.maximum(a, 0.0)).astype(BF16)
    part = _dot(a, wd_ref[...])

    @pl.when(j == 0)
    def _():
        acc_scr[...] = part

    @pl.when(j > 0)
    def _():
        acc_scr[...] += part

    @pl.when(j == pl.num_programs(1) - 1)
    def _():
        o_ref[...] = x_ref[...] + _rms(acc_scr[...], g2_ref[...])


def _mlp(x2, g1, wu, wd, g2, tm=512, tf=1024):
    m, d = x2.shape
    dff = wu.shape[1]
    assert m % tm == 0 and dff % tf == 0
    return pl.pallas_call(
        _mlp_kernel,
        grid=(m // tm, dff // tf),
        in_specs=[pl.BlockSpec((tm, d), lambda i, j: (i, 0)),
                  pl.BlockSpec((1, d), lambda i, j: (0, 0)),
                  pl.BlockSpec((d, tf), lambda i, j: (0, j)),
                  pl.BlockSpec((tf, d), lambda i, j: (j, 0)),
                  pl.BlockSpec((1, d), lambda i, j: (0, 0))],
        out_specs=pl.BlockSpec((tm, d), lambda i, j: (i, 0)),
        out_shape=jax.ShapeDtypeStruct((m, d), F32),
        scratch_shapes=[pltpu.VMEM((tm, d), BF16), pltpu.VMEM((tm, d), F32)],
        compiler_params=_params(("parallel", "arbitrary")),
        name="mlp",
    )(x2, g1, wu, wd, g2)


def _ple_kernel(x_ref, p_ref, g1_ref, wg_ref, wp_ref, g2_ref, o_ref):
    x = x_ref[...]
    gate = _sigmoid(_dot(_rms(x, g1_ref[...]).astype(BF16), wg_ref[...]))
    ple = _dot(p_ref[...].astype(BF16), wp_ref[...]) * gate
    o_ref[...] = x + _rms(ple, g2_ref[...])


def _ple(x2, p2, g1, wg, wp, g2, tm=512):
    m, d = x2.shape
    pd = p2.shape[1]
    assert m % tm == 0
    return pl.pallas_call(
        _ple_kernel,
        grid=(m // tm,),
        in_specs=[pl.BlockSpec((tm, d), lambda i: (i, 0)),
                  pl.BlockSpec((tm, pd), lambda i: (i, 0)),
                  _resident((1, d)), _resident((d, d)), _resident((pd, d)), _resident((1, d))],
        out_specs=pl.BlockSpec((tm, d), lambda i: (i, 0)),
        out_shape=jax.ShapeDtypeStruct((m, d), F32),
        compiler_params=_params(("parallel",)),
        name="ple",
    )(x2, p2, g1, wg, wp, g2)


def _prep_weights(w_in, b_if, conv_w, conv_b, ml_norm, w_br_a, w_br_b, w_out, g_pre_mix, g_post_mix,
                  g_pre_mlp, g_post_mlp, w_up, w_down, g_pre_ple, g_post_ple, w_ple, w_ple_gate):
    d = w_in.shape[0]
    mix = ml_norm.shape[0]
    H = ML_HEADS
    o_if = 4 * mix
    o_sb = o_if + 2 * H
    o_gate = o_sb + 3 * mix
    w_main = jnp.concatenate([w_in[:, :2 * mix], w_in[:, o_gate:], w_in[:, 2 * mix:o_if],
                              w_in[:, o_sb:o_gate]], axis=1).astype(BF16)
    zpad = jnp.zeros((d, LANES - H), w_in.dtype)
    w_if = jnp.concatenate([w_in[:, o_if:o_if + H], zpad, w_in[:, o_if + H:o_sb], zpad],
                           axis=1).astype(BF16)
    bpad = jnp.zeros((LANES - H,), F32)
    bif = jnp.concatenate([b_if[:H], bpad, b_if[H:], bpad])[None, :]
    row = lambda a: a[None, :].astype(F32)
    return dict(w_main=w_main, w_if=w_if, bif=bif, cw=conv_w.astype(F32), cb=row(conv_b),
                mln=row(ml_norm), wa=w_br_a.astype(BF16), wb=w_br_b.astype(BF16),
                wo=w_out.astype(BF16), g_pre_mix=row(g_pre_mix), g_post_mix=row(g_post_mix),
                g_pre_mlp=row(g_pre_mlp), g_post_mlp=row(g_post_mlp), wu=w_up.astype(BF16),
                wd=w_down.astype(BF16), g_pre_ple=row(g_pre_ple), g_post_ple=row(g_post_ple),
                wp=w_ple.astype(BF16), wg=w_ple_gate.astype(BF16))


def _layer(x, p, conv_buf, c0, n0, m0, kv_past, W):
    bsz, t, d = x.shape
    m_rows = bsz * t
    mix = W["mln"].shape[1]
    d_head = mix // SB_HEADS
    x2 = x.reshape(m_rows, d)
    proj, qkv, k_s, v_s, gates = _in_proj(x2, W["g_pre_mix"], W["w_main"], W["w_if"], mix)
    p3 = proj.reshape(bsz, t, -1)
    qkv3 = qkv.reshape(bsz, t, -1)
    if3 = gates.reshape(bsz, t, GATE_W)

    cst8 = jnp.pad(conv_buf.astype(F32), ((0, 0), (SUBLANES - (CONV_W - 1), 0), (0, 0)))
    m0p = jnp.pad(m0.astype(F32), ((0, 0), (0, LANES - ML_HEADS)))[:, None, :]
    L = min(t, ML_CHUNK)
    h_m, c_new, n_new, m_new, conv8 = _mlstm(p3, if3, cst8, c0.astype(F32), n0.astype(F32), m0p,
                                             W["cw"], W["cb"], W["bif"], W["mln"], L)
    if kv_past is None:
        h_s = _sb_prompt(qkv3, d_head)
    else:
        k_past, v_past = kv_past
        past = k_past.shape[1]
        h_s = _sb_sample(qkv3, k_past.reshape(bsz, past, mix), v_past.reshape(bsz, past, mix), d_head)

    x1 = _mix_out(x2, h_m.reshape(m_rows, mix), h_s.reshape(m_rows, mix), proj,
                  W["wa"], W["wb"], W["wo"], W["g_post_mix"])
    x3 = _mlp(x1, W["g_pre_mlp"], W["wu"], W["wd"], W["g_post_mlp"])
    y = _ple(x3, p.reshape(m_rows, -1), W["g_pre_ple"], W["wg"], W["wp"], W["g_post_ple"])

    k_s = k_s.reshape(bsz, t, SB_HEADS, d_head)
    v_s = v_s.reshape(bsz, t, SB_HEADS, d_head)
    state = (k_s, v_s, conv8[:, SUBLANES - (CONV_W - 1):, :], c_new, n_new,
             m_new[:, 0, :ML_HEADS])
    return y.reshape(bsz, t, d), state


def kernel(x_prompt, x_sample, p_prompt, p_sample, cache_sb_k, cache_sb_v, state_conv,
           state_mlstm_C, state_mlstm_n, state_mlstm_m, w_in, b_if, conv_w, conv_b, ml_norm,
           w_br_a, w_br_b, w_out, g_pre_mix, g_post_mix, g_pre_mlp, g_post_mlp, w_up, w_down,
           g_pre_ple, g_post_ple, w_ple, w_ple_gate):
    depth = w_in.shape[0]
    bp = x_prompt.shape[0]
    mix = ml_norm.shape[1]
    dh = mix // ML_HEADS
    yp, ys = x_prompt, x_sample
    st_p, st_s = [], []
    for i in range(depth):
        W = _prep_weights(w_in[i], b_if[i], conv_w[i], conv_b[i], ml_norm[i], w_br_a[i],
                          w_br_b[i], w_out[i], g_pre_mix[i], g_post_mix[i], g_pre_mlp[i],
                          g_post_mlp[i], w_up[i], w_down[i], g_pre_ple[i], g_post_ple[i],
                          w_ple[i], w_ple_gate[i])
        yp, sp = _layer(yp, p_prompt[i],
                        jnp.zeros((bp, CONV_W - 1, 2 * mix), F32),
                        jnp.zeros((bp, ML_HEADS, dh, dh), F32),
                        jnp.zeros((bp, ML_HEADS, dh), F32),
                        jnp.zeros((bp, ML_HEADS), F32),
                        None, W)
        ys, ss = _layer(ys, p_sample[i], state_conv[i], state_mlstm_C[i], state_mlstm_n[i],
                        state_mlstm_m[i], (cache_sb_k[i], cache_sb_v[i]), W)
        st_p.append(sp)
        st_s.append(ss)
    stk = lambda sts, j: jnp.stack([s[j] for s in sts])
    return (yp, ys,
            stk(st_p, 0), stk(st_p, 1), stk(st_p, 2), stk(st_p, 3), stk(st_p, 4), stk(st_p, 5),
            stk(st_s, 0), stk(st_s, 1), stk(st_s, 2), stk(st_s, 3), stk(st_s, 4), stk(st_s, 5))
```

```python
import functools

import jax
import jax.numpy as jnp
from jax import lax
from jax.experimental import pallas as pl
from jax.experimental.pallas import tpu as pltpu

F32 = jnp.float32
BF16 = jnp.bfloat16
EPS = 1e-6
LOG2E = 1.4426950408889634

LANES = 128
SUBLANES = 8
VMEM_LIMIT = 56 * 1024 * 1024

ML_HEADS = 4
SB_HEADS = 8
CONV_W = 4
ML_CHUNK = 256
SB_TILE = 256
GATE_W = 2 * LANES

NT_DIMS = (((1,), (1,)), ((), ()))


def _rms(x, g):
    return x * lax.rsqrt(jnp.mean(x * x, axis=-1, keepdims=True) + EPS) * g


def _sigmoid(x):
    return 1.0 / (1.0 + jnp.exp(-x))


def _softplus(z):
    return jnp.maximum(z, 0.0) + jnp.log(1.0 + jnp.exp(-jnp.abs(z)))


def _softplus2(z):
    neg_abs = lax.bitcast_convert_type(
        lax.bitcast_convert_type(z, jnp.uint32) | jnp.uint32(0x80000000), F32)
    return jnp.maximum(z, 0.0) + jnp.log(1.0 + jnp.exp2(neg_abs)) * LOG2E


def _pad_rows(x, rows):
    if x.shape[0] == rows:
        return x
    return jnp.concatenate([x, jnp.zeros((rows - x.shape[0],) + x.shape[1:], x.dtype)], axis=0)


def _dot(a, b):
    return jnp.dot(a, b, preferred_element_type=F32)


def _params(sem):
    return pltpu.CompilerParams(dimension_semantics=sem, vmem_limit_bytes=VMEM_LIMIT)


def _resident(shape):
    return pl.BlockSpec(shape, lambda *_: (0,) * len(shape), pipeline_mode=pl.Buffered(1))


def _in_proj_kernel(x_ref, g_ref, w_ref, wif_ref, p_ref, qkv_ref, ks_ref, vs_ref, if_ref, h_scr,
                    *, n_main, q_scale):
    j = pl.program_id(1)

    @pl.when(j == 0)
    def _():
        h = _rms(x_ref[...], g_ref[...]).astype(BF16)
        h_scr[...] = h
        if_ref[...] = _dot(h, wif_ref[...])

    def tile():
        return _dot(h_scr[...], w_ref[...])

    @pl.when(j < n_main)
    def _():
        p_ref[...] = tile()

    @pl.when(j == n_main)
    def _():
        qkv_ref[...] = (tile() * q_scale).astype(BF16)

    @pl.when(j == n_main + 1)
    def _():
        ks_ref[...] = tile()
        qkv_ref[...] = ks_ref[...].astype(BF16)

    @pl.when(j == n_main + 2)
    def _():
        vs_ref[...] = tile()
        qkv_ref[...] = vs_ref[...].astype(BF16)


def _in_proj(x2, g, w_main, w_if, mix, q_scale, tm=512):
    m, d = x2.shape
    tn = mix
    assert m % tm == 0 and w_main.shape[1] % tn == 0
    n_tiles = w_main.shape[1] // tn
    n_main = n_tiles - 3
    kern = functools.partial(_in_proj_kernel, n_main=n_main, q_scale=q_scale)
    return pl.pallas_call(
        kern,
        grid=(m // tm, n_tiles),
        in_specs=[pl.BlockSpec((tm, d), lambda i, j: (i, 0)),
                  pl.BlockSpec((1, d), lambda i, j: (0, 0)),
                  pl.BlockSpec((d, tn), lambda i, j: (0, j)),
                  pl.BlockSpec((d, GATE_W), lambda i, j: (0, 0))],
        out_specs=[pl.BlockSpec((tm, tn), lambda i, j: (i, jnp.minimum(j, n_main - 1))),
                   pl.BlockSpec((tm, tn), lambda i, j: (i, jnp.maximum(j - n_main, 0))),
                   pl.BlockSpec((tm, tn), lambda i, j: (i, 0)),
                   pl.BlockSpec((tm, tn), lambda i, j: (i, 0)),
                   pl.BlockSpec((tm, GATE_W), lambda i, j: (i, 0))],
        out_shape=[jax.ShapeDtypeStruct((m, n_main * tn), F32),
                   jax.ShapeDtypeStruct((m, 3 * tn), BF16),
                   jax.ShapeDtypeStruct((m, tn), F32),
                   jax.ShapeDtypeStruct((m, tn), F32),
                   jax.ShapeDtypeStruct((m, GATE_W), F32)],
        scratch_shapes=[pltpu.VMEM((tm, d), BF16)],
        compiler_params=_params(("parallel", "arbitrary")),
        name="in_proj",
    )(x2, g, w_main, w_if)


def _split3(x):
    hi = x.astype(BF16)
    r = x - hi.astype(F32)
    mid = r.astype(BF16)
    lo = (r - mid.astype(F32)).astype(BF16)
    return hi, mid, lo


def _mlstm_kernel(qk_ref, v_ref, o_ref, if_ref, cst_ref, c0_ref, n0_ref, m0_ref,
                  cw_ref, cb_ref, bif_ref, mln_ref, tri_ref,
                  h_ref, cout_ref, nout_ref, mout_ref, cnew_ref,
                  ext_scr, c_scr, n_scr, m_scr, *, L, Lp, H, dh):
    c = pl.program_id(1)
    mix = H * dh

    @pl.when(c == 0)
    def _():
        ext_scr[0:SUBLANES, :] = cst_ref[0]
        c_scr[...] = c0_ref[0]
        n_scr[...] = n0_ref[0]
        m_scr[...] = m0_ref[0]

    u = qk_ref[0]
    ext_scr[SUBLANES:SUBLANES + L, :] = u
    conv = cb_ref[...] + cw_ref[CONV_W - 1:CONV_W, :] * u
    for j in range(CONV_W - 1):
        lo = SUBLANES - (CONV_W - 1) + j
        conv = conv + cw_ref[j:j + 1, :] * ext_scr[lo:lo + L, :]
    tail = ext_scr[L:L + SUBLANES, :]
    ext_scr[0:SUBLANES, :] = tail
    cnew_ref[0] = tail
    qk = conv * _sigmoid(conv)
    q_all = qk[:, :mix]
    k_all = qk[:, mix:] * (dh ** -0.5)

    ifp = if_ref[0] + bif_ref[...]
    ig = ifp[:, 0:LANES]
    lf = -_softplus(-ifp[:, LANES:2 * LANES])
    tri = tri_ref[...]
    hi, mid, lo3 = _split3(_pad_rows(lf, Lp))
    b = _dot(tri, hi) + _dot(tri, mid) + _dot(tri, lo3)
    g = ig - b
    g_t = _pad_rows(g, Lp).T
    m_old = m_scr[...]
    m_last = jnp.maximum(m_old, jnp.max(g, axis=0, keepdims=True))
    cdec = jnp.exp(m_old - m_last)
    m_scr[...] = b[L - 1:L, :] + m_last
    wk = jnp.exp(g - m_last)

    row = lax.broadcasted_iota(jnp.int32, (L, Lp), 0)
    col = lax.broadcasted_iota(jnp.int32, (L, Lp), 1)
    causal = col <= row

    for h in range(H):
        sl = slice(h * dh, (h + 1) * dh)
        qh = q_all[:, sl]
        kh = k_all[:, sl]
        qb = qh.astype(BF16)
        kb = _pad_rows(kh, Lp).astype(BF16)
        vb = _pad_rows(v_ref[0, :, sl], Lp).astype(BF16)
        m_h = m_old[:, h:h + 1]
        dm = jnp.where(causal, g_t[h:h + 1, :], -jnp.inf)
        m_col = jnp.maximum(jnp.max(dm, axis=1, keepdims=True), m_h)
        s = lax.dot_general(qb, kb, NT_DIMS, preferred_element_type=F32) * jnp.exp(dm - m_col)
        dec = jnp.exp(m_h - m_col)
        num = _dot(s.astype(BF16), vb) + dec * _dot(qb, c_scr[h].astype(BF16))
        qn = jnp.sum(qh * n_scr[h:h + 1, :], axis=1, keepdims=True)
        den = jnp.sum(s, axis=1, keepdims=True) + dec * qn
        m_t = b[:, h:h + 1] + m_col
        hh = num / jnp.maximum(jnp.abs(den), jnp.exp(-m_t))
        hm = _sigmoid(o_ref[0, :, sl]) * hh
        h_ref[0, :, sl] = _rms(hm, mln_ref[:, sl]).astype(h_ref.dtype)
        kw = kh * wk[:, h:h + 1]
        cd_h = cdec[:, h:h + 1]
        c_scr[h] = cd_h * c_scr[h] + _dot(_pad_rows(kw, Lp).T.astype(BF16), vb)
        n_scr[h:h + 1, :] = cd_h * n_scr[h:h + 1, :] + jnp.sum(kw, axis=0, keepdims=True)

    @pl.when(c == pl.num_programs(1) - 1)
    def _():
        cout_ref[0] = c_scr[...]
        nout_ref[0] = n_scr[...]
        mout_ref[0] = m_scr[...]


def _mlstm(p3, if3, cst8, c0, n0, m0p, cw, cb, bif, mln, L):
    bsz, t, _ = p3.shape
    H = ML_HEADS
    mix = mln.shape[1]
    dh = mix // H
    assert t % L == 0 and L % SUBLANES == 0
    nc = t // L
    Lp = max(L, LANES)
    tri = (jnp.arange(Lp)[None, :] <= jnp.arange(L)[:, None]).astype(BF16)
    kern = functools.partial(_mlstm_kernel, L=L, Lp=Lp, H=H, dh=dh)
    return pl.pallas_call(
        kern,
        grid=(bsz, nc),
        in_specs=[pl.BlockSpec((1, L, 2 * mix), lambda b, c: (b, c, 0)),
                  pl.BlockSpec((1, L, mix), lambda b, c: (b, c, 6)),
                  pl.BlockSpec((1, L, mix), lambda b, c: (b, c, 7)),
                  pl.BlockSpec((1, L, GATE_W), lambda b, c: (b, c, 0)),
                  pl.BlockSpec((1, SUBLANES, 2 * mix), lambda b, c: (b, 0, 0)),
                  pl.BlockSpec((1, H, dh, dh), lambda b, c: (b, 0, 0, 0)),
                  pl.BlockSpec((1, H, dh), lambda b, c: (b, 0, 0)),
                  pl.BlockSpec((1, 1, LANES), lambda b, c: (b, 0, 0)),
                  pl.BlockSpec((CONV_W, 2 * mix), lambda b, c: (0, 0)),
                  pl.BlockSpec((1, 2 * mix), lambda b, c: (0, 0)),
                  pl.BlockSpec((1, GATE_W), lambda b, c: (0, 0)),
                  pl.BlockSpec((1, mix), lambda b, c: (0, 0)),
                  pl.BlockSpec((L, Lp), lambda b, c: (0, 0))],
        out_specs=[pl.BlockSpec((1, L, mix), lambda b, c: (b, c, 0)),
                   pl.BlockSpec((1, H, dh, dh), lambda b, c: (b, 0, 0, 0)),
                   pl.BlockSpec((1, H, dh), lambda b, c: (b, 0, 0)),
                   pl.BlockSpec((1, 1, LANES), lambda b, c: (b, 0, 0)),
                   pl.BlockSpec((1, SUBLANES, 2 * mix), lambda b, c: (b, 0, 0))],
        out_shape=[jax.ShapeDtypeStruct((bsz, t, mix), BF16),
                   jax.ShapeDtypeStruct((bsz, H, dh, dh), F32),
                   jax.ShapeDtypeStruct((bsz, H, dh), F32),
                   jax.ShapeDtypeStruct((bsz, 1, LANES), F32),
                   jax.ShapeDtypeStruct((bsz, SUBLANES, 2 * mix), F32)],
        scratch_shapes=[pltpu.VMEM((L + SUBLANES, 2 * mix), F32),
                        pltpu.VMEM((H, dh, dh), F32),
                        pltpu.VMEM((H, dh), F32),
                        pltpu.VMEM((1, LANES), F32)],
        compiler_params=_params(("parallel", "arbitrary")),
        name="mlstm",
    )(p3, p3, p3, if3, cst8, c0, n0, m0p, cw, cb, bif, mln, tri)


def _sb_cumsum(z, u_mat, mask):
    sp = _softplus2(z)
    if mask is not None:
        sp = jnp.where(mask, sp, 0.0)
    hi = sp.astype(BF16)
    lo = (sp - hi.astype(F32)).astype(BF16)
    return _dot(hi, u_mat) + _dot(lo, u_mat)


def _sb_weights(z, r, carry, mask):
    a = jnp.exp2(z - r - carry)
    if mask is not None:
        a = jnp.where(mask, a, 0.0)
    return a.astype(BF16), carry + r[:, 0:1]


def _sb_blocks(qs, kblks, vblks, u_mat, carries, mask, group):
    n = len(qs)
    zs, rs, out = [None] * n, [None] * n, [None] * n

    def logits(h):
        zs[h] = lax.dot_general(qs[h], kblks[h], NT_DIMS, preferred_element_type=F32)

    def cumsum(h):
        rs[h] = _sb_cumsum(zs[h], u_mat, mask)

    def weights(h):
        a, car = _sb_weights(zs[h], rs[h], carries[h], mask)
        out[h] = (_dot(a, vblks[h]), car)

    stages = (logits, cumsum, weights)
    groups = [range(g, min(g + group, n)) for g in range(0, n, group)]
    for step in range(len(groups) + len(stages) - 1):
        for k, stage in enumerate(stages):
            if 0 <= step - k < len(groups):
                for h in groups[step - k]:
                    stage(h)
    return out


def _strict_causal(tq, sk):
    row = lax.broadcasted_iota(jnp.int32, (tq, sk), 0)
    col = lax.broadcasted_iota(jnp.int32, (tq, sk), 1)
    return col < row


def _sb_prompt_kernel(q_ref, k_ref, v_ref, u_ref, o_ref, z_scr, a_scr, acc_scr, car_scr,
                      *, tq, dh, hpb):
    qi = pl.program_id(2)
    u_mat = u_ref[...]
    heads = [slice(h * dh, (h + 1) * dh) for h in range(hpb)]
    qs = [q_ref[0, :, sl] for sl in heads]

    def rows(j):
        return pl.ds(pl.multiple_of(jnp.maximum(qi - j, 0) * tq, tq), tq)

    def logits(j):
        r = rows(j)
        return [lax.dot_general(qs[h], k_ref[0, r, sl], NT_DIMS, preferred_element_type=F32)
                for h, sl in enumerate(heads)]

    def weights(zs, mask):
        rs = [None] * hpb

        def cumsum(h):
            rs[h] = _sb_cumsum(zs[h], u_mat, mask)

        def finish(h):
            a_scr[h], car_scr[h] = _sb_weights(zs[h], rs[h], car_scr[h], mask)

        lag = min(2, hpb - 1)
        for step in range(hpb + lag):
            if step < hpb:
                cumsum(step)
            if step >= lag:
                finish(step - lag)

    z0 = logits(0)
    z1 = logits(1)
    for h in range(hpb):
        car_scr[h] = jnp.zeros((tq, 1), F32)
        acc_scr[h] = jnp.zeros((tq, dh), F32)
    weights(z0, _strict_causal(tq, tq))
    for h in range(hpb):
        z_scr[h] = z1[h]

    def step(i, carry):
        zn = logits(i + 2)
        r = rows(i)
        pv = [_dot(a_scr[h], v_ref[0, r, sl]) for h, sl in enumerate(heads)]
        weights(z_scr, None)
        for h in range(hpb):
            acc_scr[h] += pv[h]
            z_scr[h] = zn[h]
        return carry

    lax.fori_loop(0, qi + 1, step, 0)
    for h, sl in enumerate(heads):
        o_ref[0, :, sl] = acc_scr[h].astype(o_ref.dtype)


def _sb_prompt(qkv3, d_head, tq=SB_TILE, hpb=4):
    bsz, t, _ = qkv3.shape
    H = SB_HEADS
    assert t % tq == 0 and H % hpb == 0
    ng = H // hpb
    w = hpb * d_head
    u_mat = (jnp.arange(tq)[:, None] >= jnp.arange(tq)[None, :]).astype(BF16)
    kern = functools.partial(_sb_prompt_kernel, tq=tq, dh=d_head, hpb=hpb)
    return pl.pallas_call(
        kern,
        grid=(bsz, ng, t // tq),
        in_specs=[pl.BlockSpec((1, tq, w), lambda b, g, i: (b, i, g)),
                  pl.BlockSpec((1, t, w), lambda b, g, i: (b, 0, ng + g)),
                  pl.BlockSpec((1, t, w), lambda b, g, i: (b, 0, 2 * ng + g)),
                  pl.BlockSpec((tq, tq), lambda b, g, i: (0, 0))],
        out_specs=pl.BlockSpec((1, tq, w), lambda b, g, i: (b, i, g)),
        out_shape=jax.ShapeDtypeStruct((bsz, t, H * d_head), BF16),
        scratch_shapes=[pltpu.VMEM((hpb, tq, tq), F32),
                        pltpu.VMEM((hpb, tq, tq), BF16),
                        pltpu.VMEM((hpb, tq, d_head), F32),
                        pltpu.VMEM((hpb, tq, 1), F32)],
        compiler_params=_params(("parallel", "parallel", "arbitrary")),
        name="sb_prompt",
    )(qkv3, qkv3, qkv3, u_mat)


def _sb_sample_kernel(q_ref, kn_ref, vn_ref, ck_ref, cv_ref, un_ref, up_ref, o_ref, acc_scr, car_scr,
                      *, L, Lp, H, dh, sk, nsub, group):
    c = pl.program_id(1)
    heads = [slice(h * dh, (h + 1) * dh) for h in range(H)]
    qs = [q_ref[0, :, sl] for sl in heads]

    @pl.when(c == 0)
    def _():
        mask = _strict_causal(L, Lp)
        res = _sb_blocks(qs, [_pad_rows(kn_ref[0, :, sl], Lp) for sl in heads],
                         [_pad_rows(vn_ref[0, :, sl], Lp) for sl in heads], un_ref[...],
                         [jnp.zeros((L, 1), F32)] * H, mask, group)
        for h, (pv, car) in enumerate(res):
            acc_scr[h] = pv
            car_scr[h] = car

    u_mat = up_ref[...]

    def body(i, carry):
        off = pl.multiple_of((nsub - 1 - i) * sk * H, sk * H)
        res2 = _sb_blocks(qs,
                          [ck_ref[0, pl.ds(off + h, sk, stride=H), :].astype(BF16) for h in range(H)],
                          [cv_ref[0, pl.ds(off + h, sk, stride=H), :].astype(BF16) for h in range(H)],
                          u_mat, [car_scr[h] for h in range(H)], None, group)
        for h, (pv2, car2) in enumerate(res2):
            acc_scr[h] += pv2
            car_scr[h] = car2
        return carry

    lax.fori_loop(0, nsub, body, 0)

    @pl.when(c == pl.num_programs(1) - 1)
    def _():
        for h, sl in enumerate(heads):
            o_ref[0, :, sl] = acc_scr[h].astype(o_ref.dtype)


def _sb_sample(qkv3, ck, cv, d_head, sk=SB_TILE, chunk=1024, group=4):
    bsz, L, _ = qkv3.shape
    H = SB_HEADS
    past = ck.shape[1] // H
    chunk = min(chunk, past)
    assert past % chunk == 0 and chunk % sk == 0
    w = H * d_head
    nch = past // chunk
    Lp = max(L, LANES)
    u_new = (jnp.arange(Lp)[:, None] >= jnp.arange(Lp)[None, :]).astype(BF16)
    u_past = (jnp.arange(sk)[:, None] >= jnp.arange(sk)[None, :]).astype(BF16)
    kern = functools.partial(_sb_sample_kernel, L=L, Lp=Lp, H=H, dh=d_head, sk=sk,
                             nsub=chunk // sk, group=group)
    return pl.pallas_call(
        kern,
        grid=(bsz, nch),
        in_specs=[pl.BlockSpec((1, L, w), lambda b, c: (b, 0, 0)),
                  pl.BlockSpec((1, L, w), lambda b, c: (b, 0, 1)),
                  pl.BlockSpec((1, L, w), lambda b, c: (b, 0, 2)),
                  pl.BlockSpec((1, chunk * H, d_head), lambda b, c: (b, nch - 1 - c, 0)),
                  pl.BlockSpec((1, chunk * H, d_head), lambda b, c: (b, nch - 1 - c, 0)),
                  pl.BlockSpec((Lp, Lp), lambda b, c: (0, 0)),
                  pl.BlockSpec((sk, sk), lambda b, c: (0, 0))],
        out_specs=pl.BlockSpec((1, L, w), lambda b, c: (b, 0, 0)),
        out_shape=jax.ShapeDtypeStruct((bsz, L, w), BF16),
        scratch_shapes=[pltpu.VMEM((H, L, d_head), F32),
                        pltpu.VMEM((H, L, 1), F32)],
        compiler_params=_params(("parallel", "arbitrary")),
        name="sb_sample",
    )(qkv3, qkv3, qkv3, ck, cv, u_new, u_past)


def _mix_out_kernel(x_ref, hm_ref, hs_ref, ga_ref, gb_ref, wa_ref, wb_ref, wo_ref, g_ref, o_ref):
    ua = _dot(hm_ref[...], wa_ref[...])
    ub = _dot(hs_ref[...], wb_ref[...])
    u = _sigmoid(ga_ref[...]) * ua + _sigmoid(gb_ref[...]) * ub
    y = _dot(u.astype(BF16), wo_ref[...])
    o_ref[...] = x_ref[...] + _rms(y, g_ref[...])


def _mix_out(x2, hm, hs, p2, wa, wb, wo, g, tm=256):
    m, d = x2.shape
    mix = hm.shape[1]
    assert m % tm == 0 and d == 2 * mix
    return pl.pallas_call(
        _mix_out_kernel,
        grid=(m // tm,),
        in_specs=[pl.BlockSpec((tm, d), lambda i: (i, 0)),
                  pl.BlockSpec((tm, mix), lambda i: (i, 0)),
                  pl.BlockSpec((tm, mix), lambda i: (i, 0)),
                  pl.BlockSpec((tm, d), lambda i: (i, 1)),
                  pl.BlockSpec((tm, d), lambda i: (i, 2)),
                  _resident((mix, d)), _resident((mix, d)), _resident((d, d)),
                  _resident((1, d))],
        out_specs=pl.BlockSpec((tm, d), lambda i: (i, 0)),
        out_shape=jax.ShapeDtypeStruct((m, d), F32),
        compiler_params=_params(("parallel",)),
        name="mix_out",
    )(x2, hm, hs, p2, p2, wa, wb, wo, g)


def _mlp_kernel(x_ref, g1_ref, wu_ref, wd_ref, g2_ref, o_ref, h_scr, acc_scr):
    j = pl.program_id(1)

    @pl.when(j == 0)
    def _():
        h_scr[...] = _rms(x_ref[...], g1_ref[...]).astype(BF16)

    a = _dot(h_scr[...], wu_ref[...])
    a = jnp.square(jnp.maximum(a, 0.0)).astype(BF16)

    @pl.when(j == 0)
    def _():
        acc_scr[...] = _dot(a, wd_ref[...])

    @pl.when(j > 0)
    def _():
        acc_scr[...] += _dot(a, wd_ref[...])

    @pl.when(j == pl.num_programs(1) - 1)
    def _():
        o_ref[...] = x_ref[...] + _rms(acc_scr[...], g2_ref[...])


def _mlp(x2, g1, wu, wd, g2, tm=512, tf=1024):
    m, d = x2.shape
    dff = wu.shape[1]
    assert m % tm == 0 and dff % tf == 0
    return pl.pallas_call(
        _mlp_kernel,
        grid=(m // tm, dff // tf),
        in_specs=[pl.BlockSpec((tm, d), lambda i, j: (i, 0)),
                  pl.BlockSpec((1, d), lambda i, j: (0, 0)),
                  pl.BlockSpec((d, tf), lambda i, j: (0, j)),
                  pl.BlockSpec((tf, d), lambda i, j: (j, 0)),
                  pl.BlockSpec((1, d), lambda i, j: (0, 0))],
        out_specs=pl.BlockSpec((tm, d), lambda i, j: (i, 0)),
        out_shape=jax.ShapeDtypeStruct((m, d), F32),
        scratch_shapes=[pltpu.VMEM((tm, d), BF16), pltpu.VMEM((tm, d), F32)],
        compiler_params=_params(("parallel", "arbitrary")),
        name="mlp",
    )(x2, g1, wu, wd, g2)


def _ple_kernel(x_ref, p_ref, g1_ref, wg_ref, wp_ref, g2_ref, o_ref):
    x = x_ref[...]
    gate = _sigmoid(_dot(_rms(x, g1_ref[...]).astype(BF16), wg_ref[...]))
    ple = _dot(p_ref[...].astype(BF16), wp_ref[...]) * gate
    o_ref[...] = x + _rms(ple, g2_ref[...])


def _ple(x2, p2, g1, wg, wp, g2, tm=512):
    m, d = x2.shape
    pd = p2.shape[1]
    assert m % tm == 0
    return pl.pallas_call(
        _ple_kernel,
        grid=(m // tm,),
        in_specs=[pl.BlockSpec((tm, d), lambda i: (i, 0)),
                  pl.BlockSpec((tm, pd), lambda i: (i, 0)),
                  _resident((1, d)), _resident((d, d)), _resident((pd, d)), _resident((1, d))],
        out_specs=pl.BlockSpec((tm, d), lambda i: (i, 0)),
        out_shape=jax.ShapeDtypeStruct((m, d), F32),
        compiler_params=_params(("parallel",)),
        name="ple",
    )(x2, p2, g1, wg, wp, g2)


def _prep_weights(w_in, b_if, conv_w, conv_b, ml_norm, w_br_a, w_br_b, w_out, g_pre_mix, g_post_mix,
                  g_pre_mlp, g_post_mlp, w_up, w_down, g_pre_ple, g_post_ple, w_ple, w_ple_gate):
    d = w_in.shape[0]
    mix = ml_norm.shape[0]
    H = ML_HEADS
    o_if = 4 * mix
    o_sb = o_if + 2 * H
    o_gate = o_sb + 3 * mix
    w_main = jnp.concatenate([w_in[:, :2 * mix], w_in[:, o_gate:], w_in[:, 2 * mix:o_if],
                              w_in[:, o_sb:o_gate]], axis=1).astype(BF16)
    zpad = jnp.zeros((d, LANES - H), w_in.dtype)
    w_if = jnp.concatenate([w_in[:, o_if:o_if + H], zpad, w_in[:, o_if + H:o_sb], zpad],
                           axis=1).astype(BF16)
    bpad = jnp.zeros((LANES - H,), F32)
    bif = jnp.concatenate([b_if[:H], bpad, b_if[H:], bpad])[None, :]
    row = lambda a: a[None, :].astype(F32)
    return dict(w_main=w_main, w_if=w_if, bif=bif, cw=conv_w.astype(F32), cb=row(conv_b),
                mln=row(ml_norm), wa=w_br_a.astype(BF16), wb=w_br_b.astype(BF16),
                wo=w_out.astype(BF16), g_pre_mix=row(g_pre_mix), g_post_mix=row(g_post_mix),
                g_pre_mlp=row(g_pre_mlp), g_post_mlp=row(g_post_mlp), wu=w_up.astype(BF16),
                wd=w_down.astype(BF16), g_pre_ple=row(g_pre_ple), g_post_ple=row(g_post_ple),
                wp=w_ple.astype(BF16), wg=w_ple_gate.astype(BF16))


def _layer(x, p, conv_buf, c0, n0, m0, kv_past, W):
    bsz, t, d = x.shape
    m_rows = bsz * t
    mix = W["mln"].shape[1]
    d_head = mix // SB_HEADS
    x2 = x.reshape(m_rows, d)
    proj, qkv, k_s, v_s, gates = _in_proj(x2, W["g_pre_mix"], W["w_main"], W["w_if"], mix,
                                          d_head ** -0.5 * LOG2E)
    p3 = proj.reshape(bsz, t, -1)
    qkv3 = qkv.reshape(bsz, t, -1)
    if3 = gates.reshape(bsz, t, GATE_W)

    cst8 = jnp.pad(conv_buf.astype(F32), ((0, 0), (SUBLANES - (CONV_W - 1), 0), (0, 0)))
    m0p = jnp.pad(m0.astype(F32), ((0, 0), (0, LANES - ML_HEADS)))[:, None, :]
    L = min(t, ML_CHUNK)
    h_m, c_new, n_new, m_new, conv8 = _mlstm(p3, if3, cst8, c0.astype(F32), n0.astype(F32), m0p,
                                             W["cw"], W["cb"], W["bif"], W["mln"], L)
    if kv_past is None:
        h_s = _sb_prompt(qkv3, d_head)
    else:
        k_past, v_past = kv_past
        past = k_past.shape[1]
        h_s = _sb_sample(qkv3, k_past.reshape(bsz, past * SB_HEADS, d_head),
                         v_past.reshape(bsz, past * SB_HEADS, d_head), d_head)

    x1 = _mix_out(x2, h_m.reshape(m_rows, mix), h_s.reshape(m_rows, mix), proj,
                  W["wa"], W["wb"], W["wo"], W["g_post_mix"])
    x3 = _mlp(x1, W["g_pre_mlp"], W["wu"], W["wd"], W["g_post_mlp"])
    y = _ple(x3, p.reshape(m_rows, -1), W["g_pre_ple"], W["wg"], W["wp"], W["g_post_ple"])

    k_s = k_s.reshape(bsz, t, SB_HEADS, d_head)
    v_s = v_s.reshape(bsz, t, SB_HEADS, d_head)
    state = (k_s, v_s, conv8[:, SUBLANES - (CONV_W - 1):, :], c_new, n_new,
             m_new[:, 0, :ML_HEADS])
    return y.reshape(bsz, t, d), state


def kernel(x_prompt, x_sample, p_prompt, p_sample, cache_sb_k, cache_sb_v, state_conv,
           state_mlstm_C, state_mlstm_n, state_mlstm_m, w_in, b_if, conv_w, conv_b, ml_norm,
           w_br_a, w_br_b, w_out, g_pre_mix, g_post_mix, g_pre_mlp, g_post_mlp, w_up, w_down,
           g_pre_ple, g_post_ple, w_ple, w_ple_gate):
    depth = w_in.shape[0]
    bp = x_prompt.shape[0]
    mix = ml_norm.shape[1]
    dh = mix // ML_HEADS
    yp, ys = x_prompt, x_sample
    st_p, st_s = [], []
    for i in range(depth):
        W = _prep_weights(w_in[i], b_if[i], conv_w[i], conv_b[i], ml_norm[i], w_br_a[i],
                          w_br_b[i], w_out[i], g_pre_mix[i], g_post_mix[i], g_pre_mlp[i],
                          g_post_mlp[i], w_up[i], w_down[i], g_pre_ple[i], g_post_ple[i],
                          w_ple[i], w_ple_gate[i])
        yp, sp = _layer(yp, p_prompt[i],
                        jnp.zeros((bp, CONV_W - 1, 2 * mix), F32),
                        jnp.zeros((bp, ML_HEADS, dh, dh), F32),
                        jnp.zeros((bp, ML_HEADS, dh), F32),
                        jnp.zeros((bp, ML_HEADS), F32),
                        None, W)
        ys, ss = _layer(ys, p_sample[i], state_conv[i], state_mlstm_C[i], state_mlstm_n[i],
                        state_mlstm_m[i], (cache_sb_k[i], cache_sb_v[i]), W)
        st_p.append(sp)
        st_s.append(ss)
    stk = lambda sts, j: jnp.stack([s[j] for s in sts])
    return (yp, ys,
            stk(st_p, 0), stk(st_p, 1), stk(st_p, 2), stk(st_p, 3), stk(st_p, 4), stk(st_p, 5),
            stk(st_s, 0), stk(st_s, 1), stk(st_s, 2), stk(st_s, 3), stk(st_s, 4), stk(st_s, 5))
```

```python
import functools

import jax
import jax.numpy as jnp
from jax import lax
from jax.experimental import pallas as pl
from jax.experimental.pallas import tpu as pltpu

F32 = jnp.float32
BF16 = jnp.bfloat16
EPS = 1e-6
LOG2E = 1.4426950408889634

LANES = 128
SUBLANES = 8
VMEM_LIMIT = 56 * 1024 * 1024

ML_HEADS = 4
SB_HEADS = 8
CONV_W = 4
ML_CHUNK = 256
SB_TILE = 256
GATE_W = 2 * LANES

NT_DIMS = (((1,), (1,)), ((), ()))


def _rms(x, g):
    return x * lax.rsqrt(jnp.mean(x * x, axis=-1, keepdims=True) + EPS) * g


def _sigmoid(x):
    return 1.0 / (1.0 + jnp.exp(-x))


def _softplus(z):
    return jnp.maximum(z, 0.0) + jnp.log(1.0 + jnp.exp(-jnp.abs(z)))


def _softplus2(z):
    neg_abs = lax.bitcast_convert_type(
        lax.bitcast_convert_type(z, jnp.uint32) | jnp.uint32(0x80000000), F32)
    return jnp.maximum(z, 0.0) + jnp.log(1.0 + jnp.exp2(neg_abs)) * LOG2E


def _pad_rows(x, rows):
    if x.shape[0] == rows:
        return x
    return jnp.concatenate([x, jnp.zeros((rows - x.shape[0],) + x.shape[1:], x.dtype)], axis=0)


def _dot(a, b):
    return jnp.dot(a, b, preferred_element_type=F32)


def _params(sem):
    return pltpu.CompilerParams(dimension_semantics=sem, vmem_limit_bytes=VMEM_LIMIT)


def _resident(shape):
    return pl.BlockSpec(shape, lambda *_: (0,) * len(shape), pipeline_mode=pl.Buffered(1))


def _in_proj_kernel(x_ref, g_ref, w_ref, wif_ref, p_ref, qkv_ref, ks_ref, vs_ref, if_ref, h_scr,
                    *, n_main, r, q_scale):
    j = pl.program_id(1)

    @pl.when(j == 0)
    def _():
        h = _rms(x_ref[...], g_ref[...]).astype(BF16)
        h_scr[...] = h
        if_ref[...] = _dot(h, wif_ref[...])

    def tile():
        return _dot(h_scr[...], w_ref[...])

    @pl.when(j < n_main)
    def _():
        p_ref[...] = tile()

    @pl.when((j >= n_main) & (j < n_main + r))
    def _():
        qkv_ref[...] = (tile() * q_scale).astype(BF16)

    @pl.when((j >= n_main + r) & (j < n_main + 2 * r))
    def _():
        ks_ref[...] = tile()
        qkv_ref[...] = ks_ref[...].astype(BF16)

    @pl.when(j >= n_main + 2 * r)
    def _():
        vs_ref[...] = tile()
        qkv_ref[...] = vs_ref[...].astype(BF16)


def _in_proj(x2, g, w_main, w_if, mix, q_scale, tm=1024, tn=512):
    m, d = x2.shape
    assert m % tm == 0 and mix % tn == 0 and w_main.shape[1] % mix == 0
    r = mix // tn
    n_tiles = w_main.shape[1] // tn
    n_main = n_tiles - 3 * r
    kern = functools.partial(_in_proj_kernel, n_main=n_main, r=r, q_scale=q_scale)
    return pl.pallas_call(
        kern,
        grid=(m // tm, n_tiles),
        in_specs=[pl.BlockSpec((tm, d), lambda i, j: (i, 0), pipeline_mode=pl.Buffered(1)),
                  pl.BlockSpec((1, d), lambda i, j: (0, 0)),
                  pl.BlockSpec((d, tn), lambda i, j: (0, j)),
                  pl.BlockSpec((d, GATE_W), lambda i, j: (0, 0), pipeline_mode=pl.Buffered(1))],
        out_specs=[pl.BlockSpec((tm, tn), lambda i, j: (i, jnp.minimum(j, n_main - 1))),
                   pl.BlockSpec((tm, tn), lambda i, j: (i, jnp.maximum(j - n_main, 0))),
                   pl.BlockSpec((tm, tn), lambda i, j: (i, jnp.clip(j - n_main - r, 0, r - 1))),
                   pl.BlockSpec((tm, tn), lambda i, j: (i, jnp.clip(j - n_main - 2 * r, 0, r - 1))),
                   pl.BlockSpec((tm, GATE_W), lambda i, j: (i, 0))],
        out_shape=[jax.ShapeDtypeStruct((m, n_main * tn), F32),
                   jax.ShapeDtypeStruct((m, 3 * mix), BF16),
                   jax.ShapeDtypeStruct((m, mix), F32),
                   jax.ShapeDtypeStruct((m, mix), F32),
                   jax.ShapeDtypeStruct((m, GATE_W), F32)],
        scratch_shapes=[pltpu.VMEM((tm, d), BF16)],
        compiler_params=_params(("parallel", "arbitrary")),
        name="in_proj",
    )(x2, g, w_main, w_if)


def _split3(x):
    hi = x.astype(BF16)
    r = x - hi.astype(F32)
    mid = r.astype(BF16)
    lo = (r - mid.astype(F32)).astype(BF16)
    return hi, mid, lo


def _mlstm_kernel(qk_ref, v_ref, o_ref, if_ref, cst_ref, c0_ref, n0_ref, m0_ref,
                  cw_ref, cb_ref, bif_ref, mln_ref, tri_ref,
                  h_ref, cout_ref, nout_ref, mout_ref, cnew_ref,
                  ext_scr, c_scr, n_scr, m_scr, *, L, Lp, H, dh):
    c = pl.program_id(1)
    mix = H * dh

    @pl.when(c == 0)
    def _():
        ext_scr[0:SUBLANES, :] = cst_ref[0]
        c_scr[...] = c0_ref[0]
        n_scr[...] = n0_ref[0]
        m_scr[...] = m0_ref[0]

    u = qk_ref[0]
    ext_scr[SUBLANES:SUBLANES + L, :] = u
    conv = cb_ref[...] + cw_ref[CONV_W - 1:CONV_W, :] * u
    for j in range(CONV_W - 1):
        lo = SUBLANES - (CONV_W - 1) + j
        conv = conv + cw_ref[j:j + 1, :] * ext_scr[lo:lo + L, :]
    tail = ext_scr[L:L + SUBLANES, :]
    ext_scr[0:SUBLANES, :] = tail
    cnew_ref[0] = tail
    qk = conv * _sigmoid(conv)
    q_all = qk[:, :mix]
    k_all = qk[:, mix:] * (dh ** -0.5)

    ifp = if_ref[0] + bif_ref[...]
    ig = ifp[:, 0:LANES]
    lf = -_softplus(-ifp[:, LANES:2 * LANES])
    tri = tri_ref[...]
    hi, mid, lo3 = _split3(_pad_rows(lf, Lp))
    b = _dot(tri, hi) + _dot(tri, mid) + _dot(tri, lo3)
    g = ig - b
    g_t = _pad_rows(g, Lp).T
    m_old = m_scr[...]
    m_last = jnp.maximum(m_old, jnp.max(g, axis=0, keepdims=True))
    cdec = jnp.exp(m_old - m_last)
    m_scr[...] = b[L - 1:L, :] + m_last
    wk = jnp.exp(g - m_last)

    row = lax.broadcasted_iota(jnp.int32, (L, Lp), 0)
    col = lax.broadcasted_iota(jnp.int32, (L, Lp), 1)
    causal = col <= row

    for h in range(H):
        sl = slice(h * dh, (h + 1) * dh)
        qh = q_all[:, sl]
        kh = k_all[:, sl]
        qb = qh.astype(BF16)
        kb = _pad_rows(kh, Lp).astype(BF16)
        vb = _pad_rows(v_ref[0, :, sl], Lp).astype(BF16)
        m_h = m_old[:, h:h + 1]
        dm = jnp.where(causal, g_t[h:h + 1, :], -jnp.inf)
        m_col = jnp.maximum(jnp.max(dm, axis=1, keepdims=True), m_h)
        s = lax.dot_general(qb, kb, NT_DIMS, preferred_element_type=F32) * jnp.exp(dm - m_col)
        dec = jnp.exp(m_h - m_col)
        num = _dot(s.astype(BF16), vb) + dec * _dot(qb, c_scr[h].astype(BF16))
        qn = jnp.sum(qh * n_scr[h:h + 1, :], axis=1, keepdims=True)
        den = jnp.sum(s, axis=1, keepdims=True) + dec * qn
        m_t = b[:, h:h + 1] + m_col
        hh = num / jnp.maximum(jnp.abs(den), jnp.exp(-m_t))
        hm = _sigmoid(o_ref[0, :, sl]) * hh
        h_ref[0, :, sl] = _rms(hm, mln_ref[:, sl]).astype(h_ref.dtype)
        kw = kh * wk[:, h:h + 1]
        cd_h = cdec[:, h:h + 1]
        c_scr[h] = cd_h * c_scr[h] + _dot(_pad_rows(kw, Lp).T.astype(BF16), vb)
        n_scr[h:h + 1, :] = cd_h * n_scr[h:h + 1, :] + jnp.sum(kw, axis=0, keepdims=True)

    @pl.when(c == pl.num_programs(1) - 1)
    def _():
        cout_ref[0] = c_scr[...]
        nout_ref[0] = n_scr[...]
        mout_ref[0] = m_scr[...]


def _mlstm(p3, if3, cst8, c0, n0, m0p, cw, cb, bif, mln, L):
    bsz, t, _ = p3.shape
    H = ML_HEADS
    mix = mln.shape[1]
    dh = mix // H
    assert t % L == 0 and L % SUBLANES == 0
    nc = t // L
    Lp = max(L, LANES)
    tri = (jnp.arange(Lp)[None, :] <= jnp.arange(L)[:, None]).astype(BF16)
    kern = functools.partial(_mlstm_kernel, L=L, Lp=Lp, H=H, dh=dh)
    return pl.pallas_call(
        kern,
        grid=(bsz, nc),
        in_specs=[pl.BlockSpec((1, L, 2 * mix), lambda b, c: (b, c, 0)),
                  pl.BlockSpec((1, L, mix), lambda b, c: (b, c, 6)),
                  pl.BlockSpec((1, L, mix), lambda b, c: (b, c, 7)),
                  pl.BlockSpec((1, L, GATE_W), lambda b, c: (b, c, 0)),
                  pl.BlockSpec((1, SUBLANES, 2 * mix), lambda b, c: (b, 0, 0)),
                  pl.BlockSpec((1, H, dh, dh), lambda b, c: (b, 0, 0, 0)),
                  pl.BlockSpec((1, H, dh), lambda b, c: (b, 0, 0)),
                  pl.BlockSpec((1, 1, LANES), lambda b, c: (b, 0, 0)),
                  pl.BlockSpec((CONV_W, 2 * mix), lambda b, c: (0, 0)),
                  pl.BlockSpec((1, 2 * mix), lambda b, c: (0, 0)),
                  pl.BlockSpec((1, GATE_W), lambda b, c: (0, 0)),
                  pl.BlockSpec((1, mix), lambda b, c: (0, 0)),
                  pl.BlockSpec((L, Lp), lambda b, c: (0, 0))],
        out_specs=[pl.BlockSpec((1, L, mix), lambda b, c: (b, c, 0)),
                   pl.BlockSpec((1, H, dh, dh), lambda b, c: (b, 0, 0, 0)),
                   pl.BlockSpec((1, H, dh), lambda b, c: (b, 0, 0)),
                   pl.BlockSpec((1, 1, LANES), lambda b, c: (b, 0, 0)),
                   pl.BlockSpec((1, SUBLANES, 2 * mix), lambda b, c: (b, 0, 0))],
        out_shape=[jax.ShapeDtypeStruct((bsz, t, mix), BF16),
                   jax.ShapeDtypeStruct((bsz, H, dh, dh), F32),
                   jax.ShapeDtypeStruct((bsz, H, dh), F32),
                   jax.ShapeDtypeStruct((bsz, 1, LANES), F32),
                   jax.ShapeDtypeStruct((bsz, SUBLANES, 2 * mix), F32)],
        scratch_shapes=[pltpu.VMEM((L + SUBLANES, 2 * mix), F32),
                        pltpu.VMEM((H, dh, dh), F32),
                        pltpu.VMEM((H, dh), F32),
                        pltpu.VMEM((1, LANES), F32)],
        compiler_params=_params(("parallel", "arbitrary")),
        name="mlstm",
    )(p3, p3, p3, if3, cst8, c0, n0, m0p, cw, cb, bif, mln, tri)


def _sb_cumsum(z, u_mat, mask):
    sp = _softplus2(z)
    if mask is not None:
        sp = jnp.where(mask, sp, 0.0)
    hi = sp.astype(BF16)
    lo = (sp - hi.astype(F32)).astype(BF16)
    return _dot(hi, u_mat) + _dot(lo, u_mat)


def _sb_weights(z, r, carry, mask):
    a = jnp.exp2(z - r - carry)
    if mask is not None:
        a = jnp.where(mask, a, 0.0)
    return a.astype(BF16), carry + r[:, 0:1]


def _sb_blocks(qs, kblks, vblks, u_mat, carries, mask, group):
    n = len(qs)
    zs, rs, out = [None] * n, [None] * n, [None] * n

    def logits(h):
        zs[h] = lax.dot_general(qs[h], kblks[h], NT_DIMS, preferred_element_type=F32)

    def cumsum(h):
        rs[h] = _sb_cumsum(zs[h], u_mat, mask)

    def weights(h):
        a, car = _sb_weights(zs[h], rs[h], carries[h], mask)
        out[h] = (_dot(a, vblks[h]), car)

    stages = (logits, cumsum, weights)
    groups = [range(g, min(g + group, n)) for g in range(0, n, group)]
    for step in range(len(groups) + len(stages) - 1):
        for k, stage in enumerate(stages):
            if 0 <= step - k < len(groups):
                for h in groups[step - k]:
                    stage(h)
    return out


def _strict_causal(tq, sk):
    row = lax.broadcasted_iota(jnp.int32, (tq, sk), 0)
    col = lax.broadcasted_iota(jnp.int32, (tq, sk), 1)
    return col < row


def _sb_prompt_kernel(q_ref, k_ref, v_ref, u_ref, o_ref, z_scr, a_scr, acc_scr, car_scr,
                      *, tq, dh, hpb):
    qi = pl.program_id(2)
    u_mat = u_ref[...]
    heads = [slice(h * dh, (h + 1) * dh) for h in range(hpb)]
    qs = [q_ref[0, :, sl] for sl in heads]

    def rows(j):
        return pl.ds(pl.multiple_of(jnp.maximum(qi - j, 0) * tq, tq), tq)

    def logits(j):
        r = rows(j)
        return [lax.dot_general(qs[h], k_ref[0, r, sl], NT_DIMS, preferred_element_type=F32)
                for h, sl in enumerate(heads)]

    def weights(zs, mask):
        rs = [None] * hpb

        def cumsum(h):
            rs[h] = _sb_cumsum(zs[h], u_mat, mask)

        def finish(h):
            a_scr[h], car_scr[h] = _sb_weights(zs[h], rs[h], car_scr[h], mask)

        lag = min(1, hpb - 1)
        for step in range(hpb + lag):
            if step < hpb:
                cumsum(step)
            if step >= lag:
                finish(step - lag)

    z0 = logits(0)
    z1 = logits(1)
    for h in range(hpb):
        car_scr[h] = jnp.zeros((tq, 1), F32)
        acc_scr[h] = jnp.zeros((tq, dh), F32)
    weights(z0, _strict_causal(tq, tq))
    for h in range(hpb):
        z_scr[h] = z1[h]

    class _Slot:
        def __init__(self, base):
            self.base = base

        def __getitem__(self, h):
            return z_scr[self.base + h]

    def step(i, parity):
        cur = parity * hpb
        nxt = hpb - cur
        r = rows(i)
        for h, sl in enumerate(heads):
            acc_scr[h] += _dot(a_scr[h], v_ref[0, r, sl])
        zn = logits(i + 2)
        for h in range(hpb):
            z_scr[nxt + h] = zn[h]
        weights(_Slot(cur), None)

    def step_pair(p, carry):
        step(2 * p, 0)
        step(2 * p + 1, 1)
        return carry

    n_steps = qi + 1
    lax.fori_loop(0, n_steps // 2, step_pair, 0)

    @pl.when(n_steps % 2 == 1)
    def _():
        step(qi, 0)

    for h, sl in enumerate(heads):
        o_ref[0, :, sl] = acc_scr[h].astype(o_ref.dtype)


def _sb_prompt(qkv3, d_head, tq=SB_TILE, hpb=4):
    bsz, t, _ = qkv3.shape
    H = SB_HEADS
    assert t % tq == 0 and H % hpb == 0
    ng = H // hpb
    w = hpb * d_head
    u_mat = (jnp.arange(tq)[:, None] >= jnp.arange(tq)[None, :]).astype(BF16)
    kern = functools.partial(_sb_prompt_kernel, tq=tq, dh=d_head, hpb=hpb)
    return pl.pallas_call(
        kern,
        grid=(bsz, ng, t // tq),
        in_specs=[pl.BlockSpec((1, tq, w), lambda b, g, i: (b, i, g)),
                  pl.BlockSpec((1, t, w), lambda b, g, i: (b, 0, ng + g)),
                  pl.BlockSpec((1, t, w), lambda b, g, i: (b, 0, 2 * ng + g)),
                  pl.BlockSpec((tq, tq), lambda b, g, i: (0, 0))],
        out_specs=pl.BlockSpec((1, tq, w), lambda b, g, i: (b, i, g)),
        out_shape=jax.ShapeDtypeStruct((bsz, t, H * d_head), BF16),
        scratch_shapes=[pltpu.VMEM((2 * hpb, tq, tq), F32),
                        pltpu.VMEM((hpb, tq, tq), BF16),
                        pltpu.VMEM((hpb, tq, d_head), F32),
                        pltpu.VMEM((hpb, tq, 1), F32)],
        compiler_params=_params(("parallel", "parallel", "arbitrary")),
        name="sb_prompt",
    )(qkv3, qkv3, qkv3, u_mat)


def _sb_sample_kernel(q_ref, kn_ref, vn_ref, ck_ref, cv_ref, un_ref, up_ref, o_ref, acc_scr, car_scr,
                      *, L, Lp, H, dh, sk, nsub, group):
    c = pl.program_id(1)
    heads = [slice(h * dh, (h + 1) * dh) for h in range(H)]
    qs = [q_ref[0, :, sl] for sl in heads]

    @pl.when(c == 0)
    def _():
        mask = _strict_causal(L, Lp)
        res = _sb_blocks(qs, [_pad_rows(kn_ref[0, :, sl], Lp) for sl in heads],
                         [_pad_rows(vn_ref[0, :, sl], Lp) for sl in heads], un_ref[...],
                         [jnp.zeros((L, 1), F32)] * H, mask, group)
        for h, (pv, car) in enumerate(res):
            acc_scr[h] = pv
            car_scr[h] = car

    u_mat = up_ref[...]

    def body(i, carry):
        off = pl.multiple_of((nsub - 1 - i) * sk * H, sk * H)
        res2 = _sb_blocks(qs,
                          [ck_ref[0, pl.ds(off + h, sk, stride=H), :].astype(BF16) for h in range(H)],
                          [cv_ref[0, pl.ds(off + h, sk, stride=H), :].astype(BF16) for h in range(H)],
                          u_mat, [car_scr[h] for h in range(H)], None, group)
        for h, (pv2, car2) in enumerate(res2):
            acc_scr[h] += pv2
            car_scr[h] = car2
        return carry

    lax.fori_loop(0, nsub, body, 0)

    @pl.when(c == pl.num_programs(1) - 1)
    def _():
        for h, sl in enumerate(heads):
            o_ref[0, :, sl] = acc_scr[h].astype(o_ref.dtype)


def _sb_sample(qkv3, ck, cv, d_head, sk=SB_TILE, chunk=1024, group=4):
    bsz, L, _ = qkv3.shape
    H = SB_HEADS
    past = ck.shape[1] // H
    chunk = min(chunk, past)
    assert past % chunk == 0 and chunk % sk == 0
    w = H * d_head
    nch = past // chunk
    Lp = max(L, LANES)
    u_new = (jnp.arange(Lp)[:, None] >= jnp.arange(Lp)[None, :]).astype(BF16)
    u_past = (jnp.arange(sk)[:, None] >= jnp.arange(sk)[None, :]).astype(BF16)
    kern = functools.partial(_sb_sample_kernel, L=L, Lp=Lp, H=H, dh=d_head, sk=sk,
                             nsub=chunk // sk, group=group)
    return pl.pallas_call(
        kern,
        grid=(bsz, nch),
        in_specs=[pl.BlockSpec((1, L, w), lambda b, c: (b, 0, 0)),
                  pl.BlockSpec((1, L, w), lambda b, c: (b, 0, 1)),
                  pl.BlockSpec((1, L, w), lambda b, c: (b, 0, 2)),
                  pl.BlockSpec((1, chunk * H, d_head), lambda b, c: (b, nch - 1 - c, 0)),
                  pl.BlockSpec((1, chunk * H, d_head), lambda b, c: (b, nch - 1 - c, 0)),
                  pl.BlockSpec((Lp, Lp), lambda b, c: (0, 0)),
                  pl.BlockSpec((sk, sk), lambda b, c: (0, 0))],
        out_specs=pl.BlockSpec((1, L, w), lambda b, c: (b, 0, 0)),
        out_shape=jax.ShapeDtypeStruct((bsz, L, w), BF16),
        scratch_shapes=[pltpu.VMEM((H, L, d_head), F32),
                        pltpu.VMEM((H, L, 1), F32)],
        compiler_params=_params(("parallel", "arbitrary")),
        name="sb_sample",
    )(qkv3, qkv3, qkv3, ck, cv, u_new, u_past)


def _mix_out_kernel(x_ref, hm_ref, hs_ref, ga_ref, gb_ref, wa_ref, wb_ref, wo_ref, g_ref, o_ref):
    ua = _dot(hm_ref[...], wa_ref[...])
    ub = _dot(hs_ref[...], wb_ref[...])
    u = _sigmoid(ga_ref[...]) * ua + _sigmoid(gb_ref[...]) * ub
    y = _dot(u.astype(BF16), wo_ref[...])
    o_ref[...] = x_ref[...] + _rms(y, g_ref[...])


def _mix_out(x2, hm, hs, p2, wa, wb, wo, g, tm=256):
    m, d = x2.shape
    mix = hm.shape[1]
    assert m % tm == 0 and d == 2 * mix
    return pl.pallas_call(
        _mix_out_kernel,
        grid=(m // tm,),
        in_specs=[pl.BlockSpec((tm, d), lambda i: (i, 0)),
                  pl.BlockSpec((tm, mix), lambda i: (i, 0)),
                  pl.BlockSpec((tm, mix), lambda i: (i, 0)),
                  pl.BlockSpec((tm, d), lambda i: (i, 1)),
                  pl.BlockSpec((tm, d), lambda i: (i, 2)),
                  _resident((mix, d)), _resident((mix, d)), _resident((d, d)),
                  _resident((1, d))],
        out_specs=pl.BlockSpec((tm, d), lambda i: (i, 0)),
        out_shape=jax.ShapeDtypeStruct((m, d), F32),
        compiler_params=_params(("parallel",)),
        name="mix_out",
    )(x2, hm, hs, p2, p2, wa, wb, wo, g)


def _mlp_kernel(x_ref, g1_ref, wu_ref, wd_ref, g2_ref, o_ref, h_scr):
    j = pl.program_id(1)

    @pl.when(j == 0)
    def _():
        h_scr[...] = _rms(x_ref[...], g1_ref[...]).astype(BF16)

    a = _dot(h_scr[...], wu_ref[...])
    a = jnp.square(jnp.maximum(a, 0.0)).astype(BF16)

    @pl.when(j == 0)
    def _():
        o_ref[...] = _dot(a, wd_ref[...])

    @pl.when(j > 0)
    def _():
        o_ref[...] += _dot(a, wd_ref[...])

    @pl.when(j == pl.num_programs(1) - 1)
    def _():
        o_ref[...] = x_ref[...] + _rms(o_ref[...], g2_ref[...])


def _mlp(x2, g1, wu, wd, g2, tm=1024, tf=512):
    m, d = x2.shape
    dff = wu.shape[1]
    assert m % tm == 0 and dff % tf == 0
    return pl.pallas_call(
        _mlp_kernel,
        grid=(m // tm, dff // tf),
        in_specs=[pl.BlockSpec((tm, d), lambda i, j: (i, 0), pipeline_mode=pl.Buffered(1)),
                  pl.BlockSpec((1, d), lambda i, j: (0, 0)),
                  pl.BlockSpec((d, tf), lambda i, j: (0, j)),
                  pl.BlockSpec((tf, d), lambda i, j: (j, 0)),
                  pl.BlockSpec((1, d), lambda i, j: (0, 0))],
        out_specs=pl.BlockSpec((tm, d), lambda i, j: (i, 0)),
        out_shape=jax.ShapeDtypeStruct((m, d), F32),
        scratch_shapes=[pltpu.VMEM((tm, d), BF16)],
        compiler_params=_params(("parallel", "arbitrary")),
        name="mlp",
    )(x2, g1, wu, wd, g2)


def _ple_kernel(x_ref, p_ref, g1_ref, wg_ref, wp_ref, g2_ref, o_ref):
    x = x_ref[...]
    gate = _sigmoid(_dot(_rms(x, g1_ref[...]).astype(BF16), wg_ref[...]))
    ple = _dot(p_ref[...].astype(BF16), wp_ref[...]) * gate
    o_ref[...] = x + _rms(ple, g2_ref[...])


def _ple(x2, p2, g1, wg, wp, g2, tm=512):
    m, d = x2.shape
    pd = p2.shape[1]
    assert m % tm == 0
    return pl.pallas_call(
        _ple_kernel,
        grid=(m // tm,),
        in_specs=[pl.BlockSpec((tm, d), lambda i: (i, 0)),
                  pl.BlockSpec((tm, pd), lambda i: (i, 0)),
                  _resident((1, d)), _resident((d, d)), _resident((pd, d)), _resident((1, d))],
        out_specs=pl.BlockSpec((tm, d), lambda i: (i, 0)),
        out_shape=jax.ShapeDtypeStruct((m, d), F32),
        compiler_params=_params(("parallel",)),
        name="ple",
    )(x2, p2, g1, wg, wp, g2)


def _prep_weights(w_in, b_if, conv_w, conv_b, ml_norm, w_br_a, w_br_b, w_out, g_pre_mix, g_post_mix,
                  g_pre_mlp, g_post_mlp, w_up, w_down, g_pre_ple, g_post_ple, w_ple, w_ple_gate):
    d = w_in.shape[0]
    mix = ml_norm.shape[0]
    H = ML_HEADS
    o_if = 4 * mix
    o_sb = o_if + 2 * H
    o_gate = o_sb + 3 * mix
    w_main = jnp.concatenate([w_in[:, :2 * mix], w_in[:, o_gate:], w_in[:, 2 * mix:o_if],
                              w_in[:, o_sb:o_gate]], axis=1).astype(BF16)
    zpad = jnp.zeros((d, LANES - H), w_in.dtype)
    w_if = jnp.concatenate([w_in[:, o_if:o_if + H], zpad, w_in[:, o_if + H:o_sb], zpad],
                           axis=1).astype(BF16)
    bpad = jnp.zeros((LANES - H,), F32)
    bif = jnp.concatenate([b_if[:H], bpad, b_if[H:], bpad])[None, :]
    row = lambda a: a[None, :].astype(F32)
    return dict(w_main=w_main, w_if=w_if, bif=bif, cw=conv_w.astype(F32), cb=row(conv_b),
                mln=row(ml_norm), wa=w_br_a.astype(BF16), wb=w_br_b.astype(BF16),
                wo=w_out.astype(BF16), g_pre_mix=row(g_pre_mix), g_post_mix=row(g_post_mix),
                g_pre_mlp=row(g_pre_mlp), g_post_mlp=row(g_post_mlp), wu=w_up.astype(BF16),
                wd=w_down.astype(BF16), g_pre_ple=row(g_pre_ple), g_post_ple=row(g_post_ple),
                wp=w_ple.astype(BF16), wg=w_ple_gate.astype(BF16))


def _layer(x, p, conv_buf, c0, n0, m0, kv_past, W):
    bsz, t, d = x.shape
    m_rows = bsz * t
    mix = W["mln"].shape[1]
    d_head = mix // SB_HEADS
    x2 = x.reshape(m_rows, d)
    proj, qkv, k_s, v_s, gates = _in_proj(x2, W["g_pre_mix"], W["w_main"], W["w_if"], mix,
                                          d_head ** -0.5 * LOG2E)
    p3 = proj.reshape(bsz, t, -1)
    qkv3 = qkv.reshape(bsz, t, -1)
    if3 = gates.reshape(bsz, t, GATE_W)

    cst8 = jnp.pad(conv_buf.astype(F32), ((0, 0), (SUBLANES - (CONV_W - 1), 0), (0, 0)))
    m0p = jnp.pad(m0.astype(F32), ((0, 0), (0, LANES - ML_HEADS)))[:, None, :]
    L = min(t, ML_CHUNK)
    h_m, c_new, n_new, m_new, conv8 = _mlstm(p3, if3, cst8, c0.astype(F32), n0.astype(F32), m0p,
                                             W["cw"], W["cb"], W["bif"], W["mln"], L)
    if kv_past is None:
        h_s = _sb_prompt(qkv3, d_head)
    else:
        k_past, v_past = kv_past
        past = k_past.shape[1]
        h_s = _sb_sample(qkv3, k_past.reshape(bsz, past * SB_HEADS, d_head),
                         v_past.reshape(bsz, past * SB_HEADS, d_head), d_head)

    x1 = _mix_out(x2, h_m.reshape(m_rows, mix), h_s.reshape(m_rows, mix), proj,
                  W["wa"], W["wb"], W["wo"], W["g_post_mix"])
    x3 = _mlp(x1, W["g_pre_mlp"], W["wu"], W["wd"], W["g_post_mlp"])
    y = _ple(x3, p.reshape(m_rows, -1), W["g_pre_ple"], W["wg"], W["wp"], W["g_post_ple"])

    k_s = k_s.reshape(bsz, t, SB_HEADS, d_head)
    v_s = v_s.reshape(bsz, t, SB_HEADS, d_head)
    state = (k_s, v_s, conv8[:, SUBLANES - (CONV_W - 1):, :], c_new, n_new,
             m_new[:, 0, :ML_HEADS])
    return y.reshape(bsz, t, d), state


def kernel(x_prompt, x_sample, p_prompt, p_sample, cache_sb_k, cache_sb_v, state_conv,
           state_mlstm_C, state_mlstm_n, state_mlstm_m, w_in, b_if, conv_w, conv_b, ml_norm,
           w_br_a, w_br_b, w_out, g_pre_mix, g_post_mix, g_pre_mlp, g_post_mlp, w_up, w_down,
           g_pre_ple, g_post_ple, w_ple, w_ple_gate):
    depth = w_in.shape[0]
    bp = x_prompt.shape[0]
    mix = ml_norm.shape[1]
    dh = mix // ML_HEADS
    yp, ys = x_prompt, x_sample
    st_p, st_s = [], []
    for i in range(depth):
        W = _prep_weights(w_in[i], b_if[i], conv_w[i], conv_b[i], ml_norm[i], w_br_a[i],
                          w_br_b[i], w_out[i], g_pre_mix[i], g_post_mix[i], g_pre_mlp[i],
                          g_post_mlp[i], w_up[i], w_down[i], g_pre_ple[i], g_post_ple[i],
                          w_ple[i], w_ple_gate[i])
        yp, sp = _layer(yp, p_prompt[i],
                        jnp.zeros((bp, CONV_W - 1, 2 * mix), F32),
                        jnp.zeros((bp, ML_HEADS, dh, dh), F32),
                        jnp.zeros((bp, ML_HEADS, dh), F32),
                        jnp.zeros((bp, ML_HEADS), F32),
                        None, W)
        ys, ss = _layer(ys, p_sample[i], state_conv[i], state_mlstm_C[i], state_mlstm_n[i],
                        state_mlstm_m[i], (cache_sb_k[i], cache_sb_v[i]), W)
        st_p.append(sp)
        st_s.append(ss)
    stk = lambda sts, j: jnp.stack([s[j] for s in sts])
    return (yp, ys,
            stk(st_p, 0), stk(st_p, 1), stk(st_p, 2), stk(st_p, 3), stk(st_p, 4), stk(st_p, 5),
            stk(st_s, 0), stk(st_s, 1), stk(st_s, 2), stk(st_s, 3), stk(st_s, 4), stk(st_s, 5))
```

```python
import functools

import jax
import jax.numpy as jnp
from jax import lax
from jax.experimental import pallas as pl
from jax.experimental.pallas import tpu as pltpu

F32 = jnp.float32
BF16 = jnp.bfloat16
EPS = 1e-6
LOG2E = 1.4426950408889634

LANES = 128
SUBLANES = 8
VMEM_LIMIT = 56 * 1024 * 1024

ML_HEADS = 4
SB_HEADS = 8
CONV_W = 4
ML_CHUNK = 256
SB_TILE = 256
GATE_W = 2 * LANES
SB_DEAD_CARRY = 160.0

NT_DIMS = (((1,), (1,)), ((), ()))


def _rms(x, g):
    return x * lax.rsqrt(jnp.mean(x * x, axis=-1, keepdims=True) + EPS) * g


def _sigmoid(x):
    return 1.0 / (1.0 + jnp.exp(-x))


def _softplus(z):
    return jnp.maximum(z, 0.0) + jnp.log(1.0 + jnp.exp(-jnp.abs(z)))


def _softplus2(z):
    neg_abs = lax.bitcast_convert_type(
        lax.bitcast_convert_type(z, jnp.uint32) | jnp.uint32(0x80000000), F32)
    return jnp.maximum(z, 0.0) + jnp.log(1.0 + jnp.exp2(neg_abs)) * LOG2E


def _pad_rows(x, rows):
    if x.shape[0] == rows:
        return x
    return jnp.concatenate([x, jnp.zeros((rows - x.shape[0],) + x.shape[1:], x.dtype)], axis=0)


def _dot(a, b):
    return jnp.dot(a, b, preferred_element_type=F32)


def _params(sem):
    return pltpu.CompilerParams(dimension_semantics=sem, vmem_limit_bytes=VMEM_LIMIT)


def _resident(shape):
    return pl.BlockSpec(shape, lambda *_: (0,) * len(shape), pipeline_mode=pl.Buffered(1))


def _in_proj_kernel(x_ref, g_ref, w_ref, wif_ref, p_ref, qkv_ref, ks_ref, vs_ref, if_ref, h_scr,
                    *, n_main, r, q_scale):
    j = pl.program_id(1)

    @pl.when(j == 0)
    def _():
        h = _rms(x_ref[...], g_ref[...]).astype(BF16)
        h_scr[...] = h
        if_ref[...] = _dot(h, wif_ref[...])

    def tile():
        return _dot(h_scr[...], w_ref[...])

    @pl.when(j < n_main)
    def _():
        p_ref[...] = tile()

    @pl.when((j >= n_main) & (j < n_main + r))
    def _():
        qkv_ref[...] = (tile() * q_scale).astype(BF16)

    @pl.when((j >= n_main + r) & (j < n_main + 2 * r))
    def _():
        ks_ref[...] = tile()
        qkv_ref[...] = ks_ref[...].astype(BF16)

    @pl.when(j >= n_main + 2 * r)
    def _():
        vs_ref[...] = tile()
        qkv_ref[...] = vs_ref[...].astype(BF16)


def _in_proj(x2, g, w_main, w_if, mix, q_scale, tm=1024, tn=512):
    m, d = x2.shape
    assert m % tm == 0 and mix % tn == 0 and w_main.shape[1] % mix == 0
    r = mix // tn
    n_tiles = w_main.shape[1] // tn
    n_main = n_tiles - 3 * r
    kern = functools.partial(_in_proj_kernel, n_main=n_main, r=r, q_scale=q_scale)
    return pl.pallas_call(
        kern,
        grid=(m // tm, n_tiles),
        in_specs=[pl.BlockSpec((tm, d), lambda i, j: (i, 0), pipeline_mode=pl.Buffered(1)),
                  pl.BlockSpec((1, d), lambda i, j: (0, 0)),
                  pl.BlockSpec((d, tn), lambda i, j: (0, j)),
                  pl.BlockSpec((d, GATE_W), lambda i, j: (0, 0), pipeline_mode=pl.Buffered(1))],
        out_specs=[pl.BlockSpec((tm, tn), lambda i, j: (i, jnp.minimum(j, n_main - 1))),
                   pl.BlockSpec((tm, tn), lambda i, j: (i, jnp.maximum(j - n_main, 0))),
                   pl.BlockSpec((tm, tn), lambda i, j: (i, jnp.clip(j - n_main - r, 0, r - 1))),
                   pl.BlockSpec((tm, tn), lambda i, j: (i, jnp.clip(j - n_main - 2 * r, 0, r - 1))),
                   pl.BlockSpec((tm, GATE_W), lambda i, j: (i, 0))],
        out_shape=[jax.ShapeDtypeStruct((m, n_main * tn), F32),
                   jax.ShapeDtypeStruct((m, 3 * mix), BF16),
                   jax.ShapeDtypeStruct((m, mix), F32),
                   jax.ShapeDtypeStruct((m, mix), F32),
                   jax.ShapeDtypeStruct((m, GATE_W), F32)],
        scratch_shapes=[pltpu.VMEM((tm, d), BF16)],
        compiler_params=_params(("parallel", "arbitrary")),
        name="in_proj",
    )(x2, g, w_main, w_if)


def _split3(x):
    hi = x.astype(BF16)
    r = x - hi.astype(F32)
    mid = r.astype(BF16)
    lo = (r - mid.astype(F32)).astype(BF16)
    return hi, mid, lo


def _mlstm_kernel(qk_ref, v_ref, o_ref, if_ref, cst_ref, c0_ref, n0_ref, m0_ref,
                  cw_ref, cb_ref, bif_ref, mln_ref, tri_ref,
                  h_ref, cout_ref, nout_ref, mout_ref, cnew_ref,
                  ext_scr, c_scr, n_scr, m_scr, *, L, Lp, H, dh):
    c = pl.program_id(1)
    mix = H * dh

    @pl.when(c == 0)
    def _():
        ext_scr[0:SUBLANES, :] = cst_ref[0]
        c_scr[...] = c0_ref[0]
        n_scr[...] = n0_ref[0]
        m_scr[...] = m0_ref[0]

    u = qk_ref[0]
    ext_scr[SUBLANES:SUBLANES + L, :] = u
    conv = cb_ref[...] + cw_ref[CONV_W - 1:CONV_W, :] * u
    for j in range(CONV_W - 1):
        lo = SUBLANES - (CONV_W - 1) + j
        conv = conv + cw_ref[j:j + 1, :] * ext_scr[lo:lo + L, :]
    tail = ext_scr[L:L + SUBLANES, :]
    ext_scr[0:SUBLANES, :] = tail
    cnew_ref[0] = tail
    qk = conv * _sigmoid(conv)
    q_all = qk[:, :mix]
    k_all = qk[:, mix:] * (dh ** -0.5)

    ifp = if_ref[0] + bif_ref[...]
    ig = ifp[:, 0:LANES]
    lf = -_softplus(-ifp[:, LANES:2 * LANES])
    tri = tri_ref[...]
    hi, mid, lo3 = _split3(_pad_rows(lf, Lp))
    b = _dot(tri, hi) + _dot(tri, mid) + _dot(tri, lo3)
    g = ig - b
    g_t = _pad_rows(g, Lp).T
    m_old = m_scr[...]
    m_last = jnp.maximum(m_old, jnp.max(g, axis=0, keepdims=True))
    cdec = jnp.exp(m_old - m_last)
    m_scr[...] = b[L - 1:L, :] + m_last
    wk = jnp.exp(g - m_last)

    row = lax.broadcasted_iota(jnp.int32, (L, Lp), 0)
    col = lax.broadcasted_iota(jnp.int32, (L, Lp), 1)
    causal = col <= row

    for h in range(H):
        sl = slice(h * dh, (h + 1) * dh)
        qh = q_all[:, sl]
        kh = k_all[:, sl]
        qb = qh.astype(BF16)
        kb = _pad_rows(kh, Lp).astype(BF16)
        vb = _pad_rows(v_ref[0, :, sl], Lp).astype(BF16)
        m_h = m_old[:, h:h + 1]
        dm = jnp.where(causal, g_t[h:h + 1, :], -jnp.inf)
        m_col = jnp.maximum(jnp.max(dm, axis=1, keepdims=True), m_h)
        s = lax.dot_general(qb, kb, NT_DIMS, preferred_element_type=F32) * jnp.exp(dm - m_col)
        dec = jnp.exp(m_h - m_col)
        num = _dot(s.astype(BF16), vb) + dec * _dot(qb, c_scr[h].astype(BF16))
        qn = jnp.sum(qh * n_scr[h:h + 1, :], axis=1, keepdims=True)
        den = jnp.sum(s, axis=1, keepdims=True) + dec * qn
        m_t = b[:, h:h + 1] + m_col
        hh = num / jnp.maximum(jnp.abs(den), jnp.exp(-m_t))
        hm = _sigmoid(o_ref[0, :, sl]) * hh
        h_ref[0, :, sl] = _rms(hm, mln_ref[:, sl]).astype(h_ref.dtype)
        kw = kh * wk[:, h:h + 1]
        cd_h = cdec[:, h:h + 1]
        c_scr[h] = cd_h * c_scr[h] + _dot(_pad_rows(kw, Lp).T.astype(BF16), vb)
        n_scr[h:h + 1, :] = cd_h * n_scr[h:h + 1, :] + jnp.sum(kw, axis=0, keepdims=True)

    @pl.when(c == pl.num_programs(1) - 1)
    def _():
        cout_ref[0] = c_scr[...]
        nout_ref[0] = n_scr[...]
        mout_ref[0] = m_scr[...]


def _mlstm(p3, if3, cst8, c0, n0, m0p, cw, cb, bif, mln, L):
    bsz, t, _ = p3.shape
    H = ML_HEADS
    mix = mln.shape[1]
    dh = mix // H
    assert t % L == 0 and L % SUBLANES == 0
    nc = t // L
    Lp = max(L, LANES)
    tri = (jnp.arange(Lp)[None, :] <= jnp.arange(L)[:, None]).astype(BF16)
    kern = functools.partial(_mlstm_kernel, L=L, Lp=Lp, H=H, dh=dh)
    return pl.pallas_call(
        kern,
        grid=(bsz, nc),
        in_specs=[pl.BlockSpec((1, L, 2 * mix), lambda b, c: (b, c, 0)),
                  pl.BlockSpec((1, L, mix), lambda b, c: (b, c, 6)),
                  pl.BlockSpec((1, L, mix), lambda b, c: (b, c, 7)),
                  pl.BlockSpec((1, L, GATE_W), lambda b, c: (b, c, 0)),
                  pl.BlockSpec((1, SUBLANES, 2 * mix), lambda b, c: (b, 0, 0)),
                  pl.BlockSpec((1, H, dh, dh), lambda b, c: (b, 0, 0, 0)),
                  pl.BlockSpec((1, H, dh), lambda b, c: (b, 0, 0)),
                  pl.BlockSpec((1, 1, LANES), lambda b, c: (b, 0, 0)),
                  pl.BlockSpec((CONV_W, 2 * mix), lambda b, c: (0, 0)),
                  pl.BlockSpec((1, 2 * mix), lambda b, c: (0, 0)),
                  pl.BlockSpec((1, GATE_W), lambda b, c: (0, 0)),
                  pl.BlockSpec((1, mix), lambda b, c: (0, 0)),
                  pl.BlockSpec((L, Lp), lambda b, c: (0, 0))],
        out_specs=[pl.BlockSpec((1, L, mix), lambda b, c: (b, c, 0)),
                   pl.BlockSpec((1, H, dh, dh), lambda b, c: (b, 0, 0, 0)),
                   pl.BlockSpec((1, H, dh), lambda b, c: (b, 0, 0)),
                   pl.BlockSpec((1, 1, LANES), lambda b, c: (b, 0, 0)),
                   pl.BlockSpec((1, SUBLANES, 2 * mix), lambda b, c: (b, 0, 0))],
        out_shape=[jax.ShapeDtypeStruct((bsz, t, mix), BF16),
                   jax.ShapeDtypeStruct((bsz, H, dh, dh), F32),
                   jax.ShapeDtypeStruct((bsz, H, dh), F32),
                   jax.ShapeDtypeStruct((bsz, 1, LANES), F32),
                   jax.ShapeDtypeStruct((bsz, SUBLANES, 2 * mix), F32)],
        scratch_shapes=[pltpu.VMEM((L + SUBLANES, 2 * mix), F32),
                        pltpu.VMEM((H, dh, dh), F32),
                        pltpu.VMEM((H, dh), F32),
                        pltpu.VMEM((1, LANES), F32)],
        compiler_params=_params(("parallel", "arbitrary")),
        name="mlstm",
    )(p3, p3, p3, if3, cst8, c0, n0, m0p, cw, cb, bif, mln, tri)


def _sb_cumsum(z, u_mat, mask):
    sp = _softplus2(z)
    if mask is not None:
        sp = jnp.where(mask, sp, 0.0)
    hi = sp.astype(BF16)
    lo = (sp - hi.astype(F32)).astype(BF16)
    return _dot(hi, u_mat) + _dot(lo, u_mat)


def _sb_weights(z, r, carry, mask):
    a = jnp.exp2(z - r - carry)
    if mask is not None:
        a = jnp.where(mask, a, 0.0)
    return a.astype(BF16), carry + r[:, 0:1]


def _sb_blocks(qs, kblks, vblks, u_mat, carries, mask, group):
    n = len(qs)
    zs, rs, out = [None] * n, [None] * n, [None] * n

    def logits(h):
        zs[h] = lax.dot_general(qs[h], kblks[h], NT_DIMS, preferred_element_type=F32)

    def cumsum(h):
        rs[h] = _sb_cumsum(zs[h], u_mat, mask)

    def weights(h):
        a, car = _sb_weights(zs[h], rs[h], carries[h], mask)
        out[h] = (_dot(a, vblks[h]), car)

    stages = (logits, cumsum, weights)
    groups = [range(g, min(g + group, n)) for g in range(0, n, group)]
    for step in range(len(groups) + len(stages) - 1):
        for k, stage in enumerate(stages):
            if 0 <= step - k < len(groups):
                for h in groups[step - k]:
                    stage(h)
    return out


def _strict_causal(tq, sk):
    row = lax.broadcasted_iota(jnp.int32, (tq, sk), 0)
    col = lax.broadcasted_iota(jnp.int32, (tq, sk), 1)
    return col < row


def _sb_prompt_kernel(q_ref, k_ref, v_ref, u_ref, o_ref, z_scr, a_scr, acc_scr, car_scr,
                      *, tq, dh, hpb):
    qi = pl.program_id(2)
    u_mat = u_ref[...]
    heads = [slice(h * dh, (h + 1) * dh) for h in range(hpb)]
    qs = [q_ref[0, :, sl] for sl in heads]

    def rows(j):
        return pl.ds(pl.multiple_of(jnp.maximum(qi - j, 0) * tq, tq), tq)

    def logits(j):
        r = rows(j)
        return [lax.dot_general(qs[h], k_ref[0, r, sl], NT_DIMS, preferred_element_type=F32)
                for h, sl in enumerate(heads)]

    def weights(zs, mask):
        rs = [None] * hpb

        def cumsum(h):
            rs[h] = _sb_cumsum(zs[h], u_mat, mask)

        def finish(h):
            a_scr[h], car_scr[h] = _sb_weights(zs[h], rs[h], car_scr[h], mask)

        lag = min(1, hpb - 1)
        for step in range(hpb + lag):
            if step < hpb:
                cumsum(step)
            if step >= lag:
                finish(step - lag)

    z0 = logits(0)
    z1 = logits(1)
    for h in range(hpb):
        car_scr[h] = jnp.zeros((tq, 1), F32)
        acc_scr[h] = jnp.zeros((tq, dh), F32)
    weights(z0, _strict_causal(tq, tq))
    for h in range(hpb):
        z_scr[h] = z1[h]

    class _Slot:
        def __init__(self, base):
            self.base = base

        def __getitem__(self, h):
            return z_scr[self.base + h]

    def step(i, parity):
        cur = parity * hpb
        nxt = hpb - cur
        r = rows(i)
        for h, sl in enumerate(heads):
            acc_scr[h] += _dot(a_scr[h], v_ref[0, r, sl])
        zn = logits(i + 2)
        for h in range(hpb):
            z_scr[nxt + h] = zn[h]
        weights(_Slot(cur), None)

    def min_carry():
        m = car_scr[0]
        for h in range(1, hpb):
            m = jnp.minimum(m, car_scr[h])
        return jnp.min(m)

    def more(state):
        p, mc = state
        return (2 * p + 1 < n_steps) & (mc < SB_DEAD_CARRY)

    def step_pair(state):
        p, _ = state
        step(2 * p, 0)
        step(2 * p + 1, 1)
        return p + 1, min_carry()

    n_steps = qi + 1
    p_end, _ = lax.while_loop(more, step_pair, (jnp.int32(0), min_carry()))
    last = 2 * p_end

    @pl.when(last < n_steps)
    def _():
        r = rows(last)
        for h, sl in enumerate(heads):
            acc_scr[h] += _dot(a_scr[h], v_ref[0, r, sl])

    for h, sl in enumerate(heads):
        o_ref[0, :, sl] = acc_scr[h].astype(o_ref.dtype)


def _sb_prompt(qkv3, d_head, tq=SB_TILE, hpb=4):
    bsz, t, _ = qkv3.shape
    H = SB_HEADS
    assert t % tq == 0 and H % hpb == 0
    ng = H // hpb
    w = hpb * d_head
    u_mat = (jnp.arange(tq)[:, None] >= jnp.arange(tq)[None, :]).astype(BF16)
    kern = functools.partial(_sb_prompt_kernel, tq=tq, dh=d_head, hpb=hpb)
    return pl.pallas_call(
        kern,
        grid=(bsz, ng, t // tq),
        in_specs=[pl.BlockSpec((1, tq, w), lambda b, g, i: (b, i, g)),
                  pl.BlockSpec((1, t, w), lambda b, g, i: (b, 0, ng + g)),
                  pl.BlockSpec((1, t, w), lambda b, g, i: (b, 0, 2 * ng + g)),
                  pl.BlockSpec((tq, tq), lambda b, g, i: (0, 0))],
        out_specs=pl.BlockSpec((1, tq, w), lambda b, g, i: (b, i, g)),
        out_shape=jax.ShapeDtypeStruct((bsz, t, H * d_head), BF16),
        scratch_shapes=[pltpu.VMEM((2 * hpb, tq, tq), F32),
                        pltpu.VMEM((hpb, tq, tq), BF16),
                        pltpu.VMEM((hpb, tq, d_head), F32),
                        pltpu.VMEM((hpb, tq, 1), F32)],
        compiler_params=_params(("parallel", "parallel", "arbitrary")),
        name="sb_prompt",
    )(qkv3, qkv3, qkv3, u_mat)


def _sb_sample_kernel(q_ref, kn_ref, vn_ref, ck_ref, cv_ref, un_ref, up_ref, o_ref, acc_scr, car_scr,
                      *, L, Lp, H, dh, sk, nsub, group):
    c = pl.program_id(1)
    heads = [slice(h * dh, (h + 1) * dh) for h in range(H)]
    qs = [q_ref[0, :, sl] for sl in heads]

    @pl.when(c == 0)
    def _():
        mask = _strict_causal(L, Lp)
        res = _sb_blocks(qs, [_pad_rows(kn_ref[0, :, sl], Lp) for sl in heads],
                         [_pad_rows(vn_ref[0, :, sl], Lp) for sl in heads], un_ref[...],
                         [jnp.zeros((L, 1), F32)] * H, mask, group)
        for h, (pv, car) in enumerate(res):
            acc_scr[h] = pv
            car_scr[h] = car

    u_mat = up_ref[...]

    def min_carry():
        m = car_scr[0]
        for h in range(1, H):
            m = jnp.minimum(m, car_scr[h])
        return jnp.min(m)

    def more(state):
        i, mc = state
        return (i < nsub) & (mc < SB_DEAD_CARRY)

    def body(state):
        i, _ = state
        off = pl.multiple_of((nsub - 1 - i) * sk * H, sk * H)
        res2 = _sb_blocks(qs,
                          [ck_ref[0, pl.ds(off + h, sk, stride=H), :].astype(BF16) for h in range(H)],
                          [cv_ref[0, pl.ds(off + h, sk, stride=H), :].astype(BF16) for h in range(H)],
                          u_mat, [car_scr[h] for h in range(H)], None, group)
        for h, (pv2, car2) in enumerate(res2):
            acc_scr[h] += pv2
            car_scr[h] = car2
        return i + 1, min_carry()

    lax.while_loop(more, body, (jnp.int32(0), min_carry()))

    @pl.when(c == pl.num_programs(1) - 1)
    def _():
        for h, sl in enumerate(heads):
            o_ref[0, :, sl] = acc_scr[h].astype(o_ref.dtype)


def _sb_sample(qkv3, ck, cv, d_head, sk=SB_TILE, chunk=1024, group=4):
    bsz, L, _ = qkv3.shape
    H = SB_HEADS
    past = ck.shape[1] // H
    chunk = min(chunk, past)
    assert past % chunk == 0 and chunk % sk == 0
    w = H * d_head
    nch = past // chunk
    Lp = max(L, LANES)
    u_new = (jnp.arange(Lp)[:, None] >= jnp.arange(Lp)[None, :]).astype(BF16)
    u_past = (jnp.arange(sk)[:, None] >= jnp.arange(sk)[None, :]).astype(BF16)
    kern = functools.partial(_sb_sample_kernel, L=L, Lp=Lp, H=H, dh=d_head, sk=sk,
                             nsub=chunk // sk, group=group)
    return pl.pallas_call(
        kern,
        grid=(bsz, nch),
        in_specs=[pl.BlockSpec((1, L, w), lambda b, c: (b, 0, 0)),
                  pl.BlockSpec((1, L, w), lambda b, c: (b, 0, 1)),
                  pl.BlockSpec((1, L, w), lambda b, c: (b, 0, 2)),
                  pl.BlockSpec((1, chunk * H, d_head), lambda b, c: (b, nch - 1 - c, 0)),
                  pl.BlockSpec((1, chunk * H, d_head), lambda b, c: (b, nch - 1 - c, 0)),
                  pl.BlockSpec((Lp, Lp), lambda b, c: (0, 0)),
                  pl.BlockSpec((sk, sk), lambda b, c: (0, 0))],
        out_specs=pl.BlockSpec((1, L, w), lambda b, c: (b, 0, 0)),
        out_shape=jax.ShapeDtypeStruct((bsz, L, w), BF16),
        scratch_shapes=[pltpu.VMEM((H, L, d_head), F32),
                        pltpu.VMEM((H, L, 1), F32)],
        compiler_params=_params(("parallel", "arbitrary")),
        name="sb_sample",
    )(qkv3, qkv3, qkv3, ck, cv, u_new, u_past)


def _mix_out_kernel(x_ref, hm_ref, hs_ref, ga_ref, gb_ref, wa_ref, wb_ref, wo_ref, g_ref, o_ref):
    ua = _dot(hm_ref[...], wa_ref[...])
    ub = _dot(hs_ref[...], wb_ref[...])
    u = _sigmoid(ga_ref[...]) * ua + _sigmoid(gb_ref[...]) * ub
    y = _dot(u.astype(BF16), wo_ref[...])
    o_ref[...] = x_ref[...] + _rms(y, g_ref[...])


def _mix_out(x2, hm, hs, p2, wa, wb, wo, g, tm=256):
    m, d = x2.shape
    mix = hm.shape[1]
    assert m % tm == 0 and d == 2 * mix
    return pl.pallas_call(
        _mix_out_kernel,
        grid=(m // tm,),
        in_specs=[pl.BlockSpec((tm, d), lambda i: (i, 0)),
                  pl.BlockSpec((tm, mix), lambda i: (i, 0)),
                  pl.BlockSpec((tm, mix), lambda i: (i, 0)),
                  pl.BlockSpec((tm, d), lambda i: (i, 1)),
                  pl.BlockSpec((tm, d), lambda i: (i, 2)),
                  _resident((mix, d)), _resident((mix, d)), _resident((d, d)),
                  _resident((1, d))],
        out_specs=pl.BlockSpec((tm, d), lambda i: (i, 0)),
        out_shape=jax.ShapeDtypeStruct((m, d), F32),
        compiler_params=_params(("parallel",)),
        name="mix_out",
    )(x2, hm, hs, p2, p2, wa, wb, wo, g)


def _mlp_kernel(x_ref, g1_ref, wu_ref, wd_ref, g2_ref, o_ref, h_scr, acc_scr):
    j = pl.program_id(1)

    @pl.when(j == 0)
    def _():
        h_scr[...] = _rms(x_ref[...], g1_ref[...]).astype(BF16)

    a = _dot(h_scr[...], wu_ref[...])
    a = jnp.square(jnp.maximum(a, 0.0)).astype(BF16)

    @pl.when(j == 0)
    def _():
        acc_scr[...] = _dot(a, wd_ref[...])

    @pl.when(j > 0)
    def _():
        acc_scr[...] += _dot(a, wd_ref[...])

    @pl.when(j == pl.num_programs(1) - 1)
    def _():
        o_ref[...] = x_ref[...] + _rms(acc_scr[...], g2_ref[...])


def _mlp(x2, g1, wu, wd, g2, tm=512, tf=1024):
    m, d = x2.shape
    dff = wu.shape[1]
    assert m % tm == 0 and dff % tf == 0
    return pl.pallas_call(
        _mlp_kernel,
        grid=(m // tm, dff // tf),
        in_specs=[pl.BlockSpec((tm, d), lambda i, j: (i, 0)),
                  pl.BlockSpec((1, d), lambda i, j: (0, 0)),
                  pl.BlockSpec((d, tf), lambda i, j: (0, j)),
                  pl.BlockSpec((tf, d), lambda i, j: (j, 0)),
                  pl.BlockSpec((1, d), lambda i, j: (0, 0))],
        out_specs=pl.BlockSpec((tm, d), lambda i, j: (i, 0)),
        out_shape=jax.ShapeDtypeStruct((m, d), F32),
        scratch_shapes=[pltpu.VMEM((tm, d), BF16), pltpu.VMEM((tm, d), F32)],
        compiler_params=_params(("parallel", "arbitrary")),
        name="mlp",
    )(x2, g1, wu, wd, g2)


def _ple_kernel(x_ref, p_ref, g1_ref, wg_ref, wp_ref, g2_ref, o_ref):
    x = x_ref[...]
    gate = _sigmoid(_dot(_rms(x, g1_ref[...]).astype(BF16), wg_ref[...]))
    ple = _dot(p_ref[...].astype(BF16), wp_ref[...]) * gate
    o_ref[...] = x + _rms(ple, g2_ref[...])


def _ple(x2, p2, g1, wg, wp, g2, tm=512):
    m, d = x2.shape
    pd = p2.shape[1]
    assert m % tm == 0
    return pl.pallas_call(
        _ple_kernel,
        grid=(m // tm,),
        in_specs=[pl.BlockSpec((tm, d), lambda i: (i, 0)),
                  pl.BlockSpec((tm, pd), lambda i: (i, 0)),
                  _resident((1, d)), _resident((d, d)), _resident((pd, d)), _resident((1, d))],
        out_specs=pl.BlockSpec((tm, d), lambda i: (i, 0)),
        out_shape=jax.ShapeDtypeStruct((m, d), F32),
        compiler_params=_params(("parallel",)),
        name="ple",
    )(x2, p2, g1, wg, wp, g2)


def _prep_weights(w_in, b_if, conv_w, conv_b, ml_norm, w_br_a, w_br_b, w_out, g_pre_mix, g_post_mix,
                  g_pre_mlp, g_post_mlp, w_up, w_down, g_pre_ple, g_post_ple, w_ple, w_ple_gate):
    d = w_in.shape[0]
    mix = ml_norm.shape[0]
    H = ML_HEADS
    o_if = 4 * mix
    o_sb = o_if + 2 * H
    o_gate = o_sb + 3 * mix
    w_main = jnp.concatenate([w_in[:, :2 * mix], w_in[:, o_gate:], w_in[:, 2 * mix:o_if],
                              w_in[:, o_sb:o_gate]], axis=1).astype(BF16)
    zpad = jnp.zeros((d, LANES - H), w_in.dtype)
    w_if = jnp.concatenate([w_in[:, o_if:o_if + H], zpad, w_in[:, o_if + H:o_sb], zpad],
                           axis=1).astype(BF16)
    bpad = jnp.zeros((LANES - H,), F32)
    bif = jnp.concatenate([b_if[:H], bpad, b_if[H:], bpad])[None, :]
    row = lambda a: a[None, :].astype(F32)
    return dict(w_main=w_main, w_if=w_if, bif=bif, cw=conv_w.astype(F32), cb=row(conv_b),
                mln=row(ml_norm), wa=w_br_a.astype(BF16), wb=w_br_b.astype(BF16),
                wo=w_out.astype(BF16), g_pre_mix=row(g_pre_mix), g_post_mix=row(g_post_mix),
                g_pre_mlp=row(g_pre_mlp), g_post_mlp=row(g_post_mlp), wu=w_up.astype(BF16),
                wd=w_down.astype(BF16), g_pre_ple=row(g_pre_ple), g_post_ple=row(g_post_ple),
                wp=w_ple.astype(BF16), wg=w_ple_gate.astype(BF16))


def _layer(x, p, conv_buf, c0, n0, m0, kv_past, W):
    bsz, t, d = x.shape
    m_rows = bsz * t
    mix = W["mln"].shape[1]
    d_head = mix // SB_HEADS
    x2 = x.reshape(m_rows, d)
    proj, qkv, k_s, v_s, gates = _in_proj(x2, W["g_pre_mix"], W["w_main"], W["w_if"], mix,
                                          d_head ** -0.5 * LOG2E)
    p3 = proj.reshape(bsz, t, -1)
    qkv3 = qkv.reshape(bsz, t, -1)
    if3 = gates.reshape(bsz, t, GATE_W)

    cst8 = jnp.pad(conv_buf.astype(F32), ((0, 0), (SUBLANES - (CONV_W - 1), 0), (0, 0)))
    m0p = jnp.pad(m0.astype(F32), ((0, 0), (0, LANES - ML_HEADS)))[:, None, :]
    L = min(t, ML_CHUNK)
    h_m, c_new, n_new, m_new, conv8 = _mlstm(p3, if3, cst8, c0.astype(F32), n0.astype(F32), m0p,
                                             W["cw"], W["cb"], W["bif"], W["mln"], L)
    if kv_past is None:
        h_s = _sb_prompt(qkv3, d_head)
    else:
        k_past, v_past = kv_past
        past = k_past.shape[1]
        h_s = _sb_sample(qkv3, k_past.reshape(bsz, past * SB_HEADS, d_head),
                         v_past.reshape(bsz, past * SB_HEADS, d_head), d_head)

    x1 = _mix_out(x2, h_m.reshape(m_rows, mix), h_s.reshape(m_rows, mix), proj,
                  W["wa"], W["wb"], W["wo"], W["g_post_mix"])
    x3 = _mlp(x1, W["g_pre_mlp"], W["wu"], W["wd"], W["g_post_mlp"])
    y = _ple(x3, p.reshape(m_rows, -1), W["g_pre_ple"], W["wg"], W["wp"], W["g_post_ple"])

    k_s = k_s.reshape(bsz, t, SB_HEADS, d_head)
    v_s = v_s.reshape(bsz, t, SB_HEADS, d_head)
    state = (k_s, v_s, conv8[:, SUBLANES - (CONV_W - 1):, :], c_new, n_new,
             m_new[:, 0, :ML_HEADS])
    return y.reshape(bsz, t, d), state


def kernel(x_prompt, x_sample, p_prompt, p_sample, cache_sb_k, cache_sb_v, state_conv,
           state_mlstm_C, state_mlstm_n, state_mlstm_m, w_in, b_if, conv_w, conv_b, ml_norm,
           w_br_a, w_br_b, w_out, g_pre_mix, g_post_mix, g_pre_mlp, g_post_mlp, w_up, w_down,
           g_pre_ple, g_post_ple, w_ple, w_ple_gate):
    depth = w_in.shape[0]
    bp = x_prompt.shape[0]
    mix = ml_norm.shape[1]
    dh = mix // ML_HEADS
    yp, ys = x_prompt, x_sample
    st_p, st_s = [], []
    for i in range(depth):
        W = _prep_weights(w_in[i], b_if[i], conv_w[i], conv_b[i], ml_norm[i], w_br_a[i],
                          w_br_b[i], w_out[i], g_pre_mix[i], g_post_mix[i], g_pre_mlp[i],
                          g_post_mlp[i], w_up[i], w_down[i], g_pre_ple[i], g_post_ple[i],
                          w_ple[i], w_ple_gate[i])
        yp, sp = _layer(yp, p_prompt[i],
                        jnp.zeros((bp, CONV_W - 1, 2 * mix), F32),
                        jnp.zeros((bp, ML_HEADS, dh, dh), F32),
                        jnp.zeros((bp, ML_HEADS, dh), F32),
                        jnp.zeros((bp, ML_HEADS), F32),
                        None, W)
        ys, ss = _layer(ys, p_sample[i], state_conv[i], state_mlstm_C[i], state_mlstm_n[i],
                        state_mlstm_m[i], (cache_sb_k[i], cache_sb_v[i]), W)
        st_p.append(sp)
        st_s.append(ss)
    stk = lambda sts, j: jnp.stack([s[j] for s in sts])
    return (yp, ys,
            stk(st_p, 0), stk(st_p, 1), stk(st_p, 2), stk(st_p, 3), stk(st_p, 4), stk(st_p, 5),
            stk(st_s, 0), stk(st_s, 1), stk(st_s, 2), stk(st_s, 3), stk(st_s, 4), stk(st_s, 5))
```

```python
import functools

import jax
import jax.numpy as jnp
from jax import lax
from jax.experimental import pallas as pl
from jax.experimental.pallas import tpu as pltpu

F32 = jnp.float32
BF16 = jnp.bfloat16
EPS = 1e-6
LOG2E = 1.4426950408889634

LANES = 128
SUBLANES = 8
VMEM_LIMIT = 56 * 1024 * 1024

ML_HEADS = 4
SB_HEADS = 8
CONV_W = 4
ML_CHUNK = 256
SB_TILE = 256
GATE_W = 2 * LANES
SB_DEAD_CARRY = 160.0

NT_DIMS = (((1,), (1,)), ((), ()))


def _rms(x, g):
    return x * lax.rsqrt(jnp.mean(x * x, axis=-1, keepdims=True) + EPS) * g


def _sigmoid(x):
    return 1.0 / (1.0 + jnp.exp(-x))


def _softplus(z):
    return jnp.maximum(z, 0.0) + jnp.log(1.0 + jnp.exp(-jnp.abs(z)))


def _softplus2(z):
    neg_abs = lax.bitcast_convert_type(
        lax.bitcast_convert_type(z, jnp.uint32) | jnp.uint32(0x80000000), F32)
    return jnp.maximum(z, 0.0) + jnp.log(1.0 + jnp.exp2(neg_abs)) * LOG2E


def _pad_rows(x, rows):
    if x.shape[0] == rows:
        return x
    return jnp.concatenate([x, jnp.zeros((rows - x.shape[0],) + x.shape[1:], x.dtype)], axis=0)


def _dot(a, b):
    return jnp.dot(a, b, preferred_element_type=F32)


def _params(sem):
    return pltpu.CompilerParams(dimension_semantics=sem, vmem_limit_bytes=VMEM_LIMIT)


def _resident(shape):
    return pl.BlockSpec(shape, lambda *_: (0,) * len(shape), pipeline_mode=pl.Buffered(1))


def _in_proj_kernel(x_ref, g_ref, w_ref, wif_ref, p_ref, qkv_ref, ks_ref, vs_ref, if_ref, h_scr,
                    *, n_main, r, q_scale):
    j = pl.program_id(1)

    @pl.when(j == 0)
    def _():
        h = _rms(x_ref[...], g_ref[...]).astype(BF16)
        h_scr[...] = h
        if_ref[...] = _dot(h, wif_ref[...])

    def tile():
        return _dot(h_scr[...], w_ref[...])

    @pl.when(j < n_main)
    def _():
        p_ref[...] = tile()

    @pl.when((j >= n_main) & (j < n_main + r))
    def _():
        qkv_ref[...] = (tile() * q_scale).astype(BF16)

    @pl.when((j >= n_main + r) & (j < n_main + 2 * r))
    def _():
        ks_ref[...] = tile()
        qkv_ref[...] = ks_ref[...].astype(BF16)

    @pl.when(j >= n_main + 2 * r)
    def _():
        vs_ref[...] = tile()
        qkv_ref[...] = vs_ref[...].astype(BF16)


def _in_proj(x2, g, w_main, w_if, mix, q_scale, tm=1024, tn=512):
    m, d = x2.shape
    assert m % tm == 0 and mix % tn == 0 and w_main.shape[1] % mix == 0
    r = mix // tn
    n_tiles = w_main.shape[1] // tn
    n_main = n_tiles - 3 * r
    kern = functools.partial(_in_proj_kernel, n_main=n_main, r=r, q_scale=q_scale)
    return pl.pallas_call(
        kern,
        grid=(m // tm, n_tiles),
        in_specs=[pl.BlockSpec((tm, d), lambda i, j: (i, 0), pipeline_mode=pl.Buffered(1)),
                  pl.BlockSpec((1, d), lambda i, j: (0, 0)),
                  pl.BlockSpec((d, tn), lambda i, j: (0, j)),
                  pl.BlockSpec((d, GATE_W), lambda i, j: (0, 0), pipeline_mode=pl.Buffered(1))],
        out_specs=[pl.BlockSpec((tm, tn), lambda i, j: (i, jnp.minimum(j, n_main - 1))),
                   pl.BlockSpec((tm, tn), lambda i, j: (i, jnp.maximum(j - n_main, 0))),
                   pl.BlockSpec((tm, tn), lambda i, j: (i, jnp.clip(j - n_main - r, 0, r - 1))),
                   pl.BlockSpec((tm, tn), lambda i, j: (i, jnp.clip(j - n_main - 2 * r, 0, r - 1))),
                   pl.BlockSpec((tm, GATE_W), lambda i, j: (i, 0))],
        out_shape=[jax.ShapeDtypeStruct((m, n_main * tn), F32),
                   jax.ShapeDtypeStruct((m, 3 * mix), BF16),
                   jax.ShapeDtypeStruct((m, mix), F32),
                   jax.ShapeDtypeStruct((m, mix), F32),
                   jax.ShapeDtypeStruct((m, GATE_W), F32)],
        scratch_shapes=[pltpu.VMEM((tm, d), BF16)],
        compiler_params=_params(("parallel", "arbitrary")),
        name="in_proj",
    )(x2, g, w_main, w_if)


def _split3(x):
    hi = x.astype(BF16)
    r = x - hi.astype(F32)
    mid = r.astype(BF16)
    lo = (r - mid.astype(F32)).astype(BF16)
    return hi, mid, lo


def _mlstm_kernel(qk_ref, v_ref, o_ref, if_ref, cst_ref, c0_ref, n0_ref, m0_ref,
                  cw_ref, cb_ref, bif_ref, mln_ref, tri_ref,
                  h_ref, cout_ref, nout_ref, mout_ref, cnew_ref,
                  ext_scr, c_scr, n_scr, m_scr, *, L, Lp, H, dh):
    c = pl.program_id(1)
    mix = H * dh

    @pl.when(c == 0)
    def _():
        ext_scr[0:SUBLANES, :] = cst_ref[0]
        c_scr[...] = c0_ref[0]
        n_scr[...] = n0_ref[0]
        m_scr[...] = m0_ref[0]

    u = qk_ref[0]
    ext_scr[SUBLANES:SUBLANES + L, :] = u
    conv = cb_ref[...] + cw_ref[CONV_W - 1:CONV_W, :] * u
    for j in range(CONV_W - 1):
        lo = SUBLANES - (CONV_W - 1) + j
        conv = conv + cw_ref[j:j + 1, :] * ext_scr[lo:lo + L, :]
    tail = ext_scr[L:L + SUBLANES, :]
    ext_scr[0:SUBLANES, :] = tail
    cnew_ref[0] = tail
    qk = conv * _sigmoid(conv)
    q_all = qk[:, :mix]
    k_all = qk[:, mix:] * (dh ** -0.5)

    ifp = if_ref[0] + bif_ref[...]
    ig = ifp[:, 0:LANES]
    lf = -_softplus(-ifp[:, LANES:2 * LANES])
    tri = tri_ref[...]
    hi, mid, lo3 = _split3(_pad_rows(lf, Lp))
    b = _dot(tri, hi) + _dot(tri, mid) + _dot(tri, lo3)
    g = ig - b
    g_t = _pad_rows(g, Lp).T
    m_old = m_scr[...]
    m_last = jnp.maximum(m_old, jnp.max(g, axis=0, keepdims=True))
    cdec = jnp.exp(m_old - m_last)
    m_scr[...] = b[L - 1:L, :] + m_last
    wk = jnp.exp(g - m_last)

    row = lax.broadcasted_iota(jnp.int32, (L, Lp), 0)
    col = lax.broadcasted_iota(jnp.int32, (L, Lp), 1)
    causal = col <= row

    for h in range(H):
        sl = slice(h * dh, (h + 1) * dh)
        qh = q_all[:, sl]
        kh = k_all[:, sl]
        qb = qh.astype(BF16)
        kb = _pad_rows(kh, Lp).astype(BF16)
        vb = _pad_rows(v_ref[0, :, sl], Lp).astype(BF16)
        m_h = m_old[:, h:h + 1]
        dm = jnp.where(causal, g_t[h:h + 1, :], -jnp.inf)
        m_col = jnp.maximum(jnp.max(dm, axis=1, keepdims=True), m_h)
        s = lax.dot_general(qb, kb, NT_DIMS, preferred_element_type=F32) * jnp.exp(dm - m_col)
        dec = jnp.exp(m_h - m_col)
        num = _dot(s.astype(BF16), vb) + dec * _dot(qb, c_scr[h].astype(BF16))
        qn = jnp.sum(qh * n_scr[h:h + 1, :], axis=1, keepdims=True)
        den = jnp.sum(s, axis=1, keepdims=True) + dec * qn
        m_t = b[:, h:h + 1] + m_col
        hh = num / jnp.maximum(jnp.abs(den), jnp.exp(-m_t))
        hm = _sigmoid(o_ref[0, :, sl]) * hh
        h_ref[0, :, sl] = _rms(hm, mln_ref[:, sl]).astype(h_ref.dtype)
        kw = kh * wk[:, h:h + 1]
        cd_h = cdec[:, h:h + 1]
        c_scr[h] = cd_h * c_scr[h] + _dot(_pad_rows(kw, Lp).T.astype(BF16), vb)
        n_scr[h:h + 1, :] = cd_h * n_scr[h:h + 1, :] + jnp.sum(kw, axis=0, keepdims=True)

    @pl.when(c == pl.num_programs(1) - 1)
    def _():
        cout_ref[0] = c_scr[...]
        nout_ref[0] = n_scr[...]
        mout_ref[0] = m_scr[...]


def _mlstm(p3, if3, cst8, c0, n0, m0p, cw, cb, bif, mln, L):
    bsz, t, _ = p3.shape
    H = ML_HEADS
    mix = mln.shape[1]
    dh = mix // H
    assert t % L == 0 and L % SUBLANES == 0
    nc = t // L
    Lp = max(L, LANES)
    tri = (jnp.arange(Lp)[None, :] <= jnp.arange(L)[:, None]).astype(BF16)
    kern = functools.partial(_mlstm_kernel, L=L, Lp=Lp, H=H, dh=dh)
    return pl.pallas_call(
        kern,
        grid=(bsz, nc),
        in_specs=[pl.BlockSpec((1, L, 2 * mix), lambda b, c: (b, c, 0)),
                  pl.BlockSpec((1, L, mix), lambda b, c: (b, c, 6)),
                  pl.BlockSpec((1, L, mix), lambda b, c: (b, c, 7)),
                  pl.BlockSpec((1, L, GATE_W), lambda b, c: (b, c, 0)),
                  pl.BlockSpec((1, SUBLANES, 2 * mix), lambda b, c: (b, 0, 0)),
                  pl.BlockSpec((1, H, dh, dh), lambda b, c: (b, 0, 0, 0)),
                  pl.BlockSpec((1, H, dh), lambda b, c: (b, 0, 0)),
                  pl.BlockSpec((1, 1, LANES), lambda b, c: (b, 0, 0)),
                  pl.BlockSpec((CONV_W, 2 * mix), lambda b, c: (0, 0)),
                  pl.BlockSpec((1, 2 * mix), lambda b, c: (0, 0)),
                  pl.BlockSpec((1, GATE_W), lambda b, c: (0, 0)),
                  pl.BlockSpec((1, mix), lambda b, c: (0, 0)),
                  pl.BlockSpec((L, Lp), lambda b, c: (0, 0))],
        out_specs=[pl.BlockSpec((1, L, mix), lambda b, c: (b, c, 0)),
                   pl.BlockSpec((1, H, dh, dh), lambda b, c: (b, 0, 0, 0)),
                   pl.BlockSpec((1, H, dh), lambda b, c: (b, 0, 0)),
                   pl.BlockSpec((1, 1, LANES), lambda b, c: (b, 0, 0)),
                   pl.BlockSpec((1, SUBLANES, 2 * mix), lambda b, c: (b, 0, 0))],
        out_shape=[jax.ShapeDtypeStruct((bsz, t, mix), BF16),
                   jax.ShapeDtypeStruct((bsz, H, dh, dh), F32),
                   jax.ShapeDtypeStruct((bsz, H, dh), F32),
                   jax.ShapeDtypeStruct((bsz, 1, LANES), F32),
                   jax.ShapeDtypeStruct((bsz, SUBLANES, 2 * mix), F32)],
        scratch_shapes=[pltpu.VMEM((L + SUBLANES, 2 * mix), F32),
                        pltpu.VMEM((H, dh, dh), F32),
                        pltpu.VMEM((H, dh), F32),
                        pltpu.VMEM((1, LANES), F32)],
        compiler_params=_params(("parallel", "arbitrary")),
        name="mlstm",
    )(p3, p3, p3, if3, cst8, c0, n0, m0p, cw, cb, bif, mln, tri)


def _sb_cumsum(z, u_mat, mask):
    sp = _softplus2(z)
    if mask is not None:
        sp = jnp.where(mask, sp, 0.0)
    hi = sp.astype(BF16)
    lo = (sp - hi.astype(F32)).astype(BF16)
    return _dot(hi, u_mat) + _dot(lo, u_mat)


def _sb_weights(z, r, carry, mask):
    a = jnp.exp2(z - r - carry)
    if mask is not None:
        a = jnp.where(mask, a, 0.0)
    return a.astype(BF16), carry + r[:, 0:1]


def _sb_blocks(qs, kblks, vblks, u_mat, carries, mask, group):
    n = len(qs)
    zs, rs, out = [None] * n, [None] * n, [None] * n

    def logits(h):
        zs[h] = lax.dot_general(qs[h], kblks[h], NT_DIMS, preferred_element_type=F32)

    def cumsum(h):
        rs[h] = _sb_cumsum(zs[h], u_mat, mask)

    def weights(h):
        a, car = _sb_weights(zs[h], rs[h], carries[h], mask)
        out[h] = (_dot(a, vblks[h]), car)

    stages = (logits, cumsum, weights)
    groups = [range(g, min(g + group, n)) for g in range(0, n, group)]
    for step in range(len(groups) + len(stages) - 1):
        for k, stage in enumerate(stages):
            if 0 <= step - k < len(groups):
                for h in groups[step - k]:
                    stage(h)
    return out


def _strict_causal(tq, sk):
    row = lax.broadcasted_iota(jnp.int32, (tq, sk), 0)
    col = lax.broadcasted_iota(jnp.int32, (tq, sk), 1)
    return col < row


def _sb_prompt_kernel(q_ref, k_ref, v_ref, u_ref, o_ref, z_scr, a_scr, acc_scr, car_scr,
                      *, tq, dh, hpb):
    qi = pl.program_id(2)
    u_mat = u_ref[...]
    heads = [slice(h * dh, (h + 1) * dh) for h in range(hpb)]
    qs = [q_ref[0, :, sl] for sl in heads]

    def rows(j):
        return pl.ds(pl.multiple_of(jnp.maximum(qi - j, 0) * tq, tq), tq)

    def logits(j):
        r = rows(j)
        return [lax.dot_general(qs[h], k_ref[0, r, sl], NT_DIMS, preferred_element_type=F32)
                for h, sl in enumerate(heads)]

    def weights(zs, mask):
        rs = [None] * hpb

        def cumsum(h):
            rs[h] = _sb_cumsum(zs[h], u_mat, mask)

        def finish(h):
            a_scr[h], car_scr[h] = _sb_weights(zs[h], rs[h], car_scr[h], mask)

        lag = min(1, hpb - 1)
        for step in range(hpb + lag):
            if step < hpb:
                cumsum(step)
            if step >= lag:
                finish(step - lag)

    z0 = logits(0)
    z1 = logits(1)
    for h in range(hpb):
        car_scr[h] = jnp.zeros((tq, 1), F32)
        acc_scr[h] = jnp.zeros((tq, dh), F32)
    weights(z0, _strict_causal(tq, tq))
    for h in range(hpb):
        z_scr[h] = z1[h]

    def apply_weights(i):
        r = rows(i)
        for h, sl in enumerate(heads):
            acc_scr[h] += _dot(a_scr[h], v_ref[0, r, sl])

    def step(i):
        apply_weights(i)
        zn = logits(i + 2)
        for h in range(hpb):
            z_scr[hpb + h] = zn[h]
        weights(z_scr, None)
        for h in range(hpb):
            z_scr[h] = z_scr[hpb + h]

    def min_carry():
        m = car_scr[0]
        for h in range(1, hpb):
            m = jnp.minimum(m, car_scr[h])
        return jnp.min(m)

    def more(state):
        i, mc = state
        return (i < qi) & (mc < SB_DEAD_CARRY)

    def body(state):
        i, _ = state
        step(i)
        return i + 1, min_carry()

    last, _ = lax.while_loop(more, body, (jnp.int32(0), min_carry()))
    apply_weights(last)

    for h, sl in enumerate(heads):
        o_ref[0, :, sl] = acc_scr[h].astype(o_ref.dtype)


def _sb_prompt(qkv3, d_head, tq=SB_TILE, hpb=4):
    bsz, t, _ = qkv3.shape
    H = SB_HEADS
    assert t % tq == 0 and H % hpb == 0
    ng = H // hpb
    w = hpb * d_head
    u_mat = (jnp.arange(tq)[:, None] >= jnp.arange(tq)[None, :]).astype(BF16)
    kern = functools.partial(_sb_prompt_kernel, tq=tq, dh=d_head, hpb=hpb)
    return pl.pallas_call(
        kern,
        grid=(bsz, ng, t // tq),
        in_specs=[pl.BlockSpec((1, tq, w), lambda b, g, i: (b, i, g)),
                  pl.BlockSpec((1, t, w), lambda b, g, i: (b, 0, ng + g)),
                  pl.BlockSpec((1, t, w), lambda b, g, i: (b, 0, 2 * ng + g)),
                  pl.BlockSpec((tq, tq), lambda b, g, i: (0, 0))],
        out_specs=pl.BlockSpec((1, tq, w), lambda b, g, i: (b, i, g)),
        out_shape=jax.ShapeDtypeStruct((bsz, t, H * d_head), BF16),
        scratch_shapes=[pltpu.VMEM((2 * hpb, tq, tq), F32),
                        pltpu.VMEM((hpb, tq, tq), BF16),
                        pltpu.VMEM((hpb, tq, d_head), F32),
                        pltpu.VMEM((hpb, tq, 1), F32)],
        compiler_params=_params(("parallel", "parallel", "arbitrary")),
        name="sb_prompt",
    )(qkv3, qkv3, qkv3, u_mat)


def _sb_sample_kernel(q_ref, kn_ref, vn_ref, ck_ref, cv_ref, un_ref, up_ref, o_ref, acc_scr, car_scr,
                      *, L, Lp, H, dh, sk, nsub, group):
    c = pl.program_id(1)
    heads = [slice(h * dh, (h + 1) * dh) for h in range(H)]
    qs = [q_ref[0, :, sl] for sl in heads]

    @pl.when(c == 0)
    def _():
        mask = _strict_causal(L, Lp)
        res = _sb_blocks(qs, [_pad_rows(kn_ref[0, :, sl], Lp) for sl in heads],
                         [_pad_rows(vn_ref[0, :, sl], Lp) for sl in heads], un_ref[...],
                         [jnp.zeros((L, 1), F32)] * H, mask, group)
        for h, (pv, car) in enumerate(res):
            acc_scr[h] = pv
            car_scr[h] = car

    u_mat = up_ref[...]

    def min_carry():
        m = car_scr[0]
        for h in range(1, H):
            m = jnp.minimum(m, car_scr[h])
        return jnp.min(m)

    def more(state):
        i, mc = state
        return (i < nsub) & (mc < SB_DEAD_CARRY)

    def body(state):
        i, _ = state
        off = pl.multiple_of((nsub - 1 - i) * sk * H, sk * H)
        res2 = _sb_blocks(qs,
                          [ck_ref[0, pl.ds(off + h, sk, stride=H), :].astype(BF16) for h in range(H)],
                          [cv_ref[0, pl.ds(off + h, sk, stride=H), :].astype(BF16) for h in range(H)],
                          u_mat, [car_scr[h] for h in range(H)], None, group)
        for h, (pv2, car2) in enumerate(res2):
            acc_scr[h] += pv2
            car_scr[h] = car2
        return i + 1, min_carry()

    lax.while_loop(more, body, (jnp.int32(0), min_carry()))

    @pl.when(c == pl.num_programs(1) - 1)
    def _():
        for h, sl in enumerate(heads):
            o_ref[0, :, sl] = acc_scr[h].astype(o_ref.dtype)


def _sb_sample(qkv3, ck, cv, d_head, sk=SB_TILE, chunk=1024, group=4):
    bsz, L, _ = qkv3.shape
    H = SB_HEADS
    past = ck.shape[1] // H
    chunk = min(chunk, past)
    assert past % chunk == 0 and chunk % sk == 0
    w = H * d_head
    nch = past // chunk
    Lp = max(L, LANES)
    u_new = (jnp.arange(Lp)[:, None] >= jnp.arange(Lp)[None, :]).astype(BF16)
    u_past = (jnp.arange(sk)[:, None] >= jnp.arange(sk)[None, :]).astype(BF16)
    kern = functools.partial(_sb_sample_kernel, L=L, Lp=Lp, H=H, dh=d_head, sk=sk,
                             nsub=chunk // sk, group=group)
    return pl.pallas_call(
        kern,
        grid=(bsz, nch),
        in_specs=[pl.BlockSpec((1, L, w), lambda b, c: (b, 0, 0)),
                  pl.BlockSpec((1, L, w), lambda b, c: (b, 0, 1)),
                  pl.BlockSpec((1, L, w), lambda b, c: (b, 0, 2)),
                  pl.BlockSpec((1, chunk * H, d_head), lambda b, c: (b, nch - 1 - c, 0)),
                  pl.BlockSpec((1, chunk * H, d_head), lambda b, c: (b, nch - 1 - c, 0)),
                  pl.BlockSpec((Lp, Lp), lambda b, c: (0, 0)),
                  pl.BlockSpec((sk, sk), lambda b, c: (0, 0))],
        out_specs=pl.BlockSpec((1, L, w), lambda b, c: (b, 0, 0)),
        out_shape=jax.ShapeDtypeStruct((bsz, L, w), BF16),
        scratch_shapes=[pltpu.VMEM((H, L, d_head), F32),
                        pltpu.VMEM((H, L, 1), F32)],
        compiler_params=_params(("parallel", "arbitrary")),
        name="sb_sample",
    )(qkv3, qkv3, qkv3, ck, cv, u_new, u_past)


def _mix_out_kernel(x_ref, hm_ref, hs_ref, ga_ref, gb_ref, wa_ref, wb_ref, wo_ref, g_ref, o_ref):
    ua = _dot(hm_ref[...], wa_ref[...])
    ub = _dot(hs_ref[...], wb_ref[...])
    u = _sigmoid(ga_ref[...]) * ua + _sigmoid(gb_ref[...]) * ub
    y = _dot(u.astype(BF16), wo_ref[...])
    o_ref[...] = x_ref[...] + _rms(y, g_ref[...])


def _mix_out(x2, hm, hs, p2, wa, wb, wo, g, tm=256):
    m, d = x2.shape
    mix = hm.shape[1]
    assert m % tm == 0 and d == 2 * mix
    return pl.pallas_call(
        _mix_out_kernel,
        grid=(m // tm,),
        in_specs=[pl.BlockSpec((tm, d), lambda i: (i, 0)),
                  pl.BlockSpec((tm, mix), lambda i: (i, 0)),
                  pl.BlockSpec((tm, mix), lambda i: (i, 0)),
                  pl.BlockSpec((tm, d), lambda i: (i, 1)),
                  pl.BlockSpec((tm, d), lambda i: (i, 2)),
                  _resident((mix, d)), _resident((mix, d)), _resident((d, d)),
                  _resident((1, d))],
        out_specs=pl.BlockSpec((tm, d), lambda i: (i, 0)),
        out_shape=jax.ShapeDtypeStruct((m, d), F32),
        compiler_params=_params(("parallel",)),
        name="mix_out",
    )(x2, hm, hs, p2, p2, wa, wb, wo, g)


def _mlp_kernel(x_ref, g1_ref, wu_ref, wd_ref, g2_ref, o_ref, h_scr, acc_scr):
    j = pl.program_id(1)

    @pl.when(j == 0)
    def _():
        h_scr[...] = _rms(x_ref[...], g1_ref[...]).astype(BF16)

    a = _dot(h_scr[...], wu_ref[...])
    a = jnp.square(jnp.maximum(a, 0.0)).astype(BF16)

    @pl.when(j == 0)
    def _():
        acc_scr[...] = _dot(a, wd_ref[...])

    @pl.when(j > 0)
    def _():
        acc_scr[...] += _dot(a, wd_ref[...])

    @pl.when(j == pl.num_programs(1) - 1)
    def _():
        o_ref[...] = x_ref[...] + _rms(acc_scr[...], g2_ref[...])


def _mlp(x2, g1, wu, wd, g2, tm=512, tf=1024):
    m, d = x2.shape
    dff = wu.shape[1]
    assert m % tm == 0 and dff % tf == 0
    return pl.pallas_call(
        _mlp_kernel,
        grid=(m // tm, dff // tf),
        in_specs=[pl.BlockSpec((tm, d), lambda i, j: (i, 0)),
                  pl.BlockSpec((1, d), lambda i, j: (0, 0)),
                  pl.BlockSpec((d, tf), lambda i, j: (0, j)),
                  pl.BlockSpec((tf, d), lambda i, j: (j, 0)),
                  pl.BlockSpec((1, d), lambda i, j: (0, 0))],
        out_specs=pl.BlockSpec((tm, d), lambda i, j: (i, 0)),
        out_shape=jax.ShapeDtypeStruct((m, d), F32),
        scratch_shapes=[pltpu.VMEM((tm, d), BF16), pltpu.VMEM((tm, d), F32)],
        compiler_params=_params(("parallel", "arbitrary")),
        name="mlp",
    )(x2, g1, wu, wd, g2)


def _ple_kernel(x_ref, p_ref, g1_ref, wg_ref, wp_ref, g2_ref, o_ref):
    x = x_ref[...]
    gate = _sigmoid(_dot(_rms(x, g1_ref[...]).astype(BF16), wg_ref[...]))
    ple = _dot(p_ref[...].astype(BF16), wp_ref[...]) * gate
    o_ref[...] = x + _rms(ple, g2_ref[...])


def _ple(x2, p2, g1, wg, wp, g2, tm=512):
    m, d = x2.shape
    pd = p2.shape[1]
    assert m % tm == 0
    return pl.pallas_call(
        _ple_kernel,
        grid=(m // tm,),
        in_specs=[pl.BlockSpec((tm, d), lambda i: (i, 0)),
                  pl.BlockSpec((tm, pd), lambda i: (i, 0)),
                  _resident((1, d)), _resident((d, d)), _resident((pd, d)), _resident((1, d))],
        out_specs=pl.BlockSpec((tm, d), lambda i: (i, 0)),
        out_shape=jax.ShapeDtypeStruct((m, d), F32),
        compiler_params=_params(("parallel",)),
        name="ple",
    )(x2, p2, g1, wg, wp, g2)


def _prep_weights(w_in, b_if, conv_w, conv_b, ml_norm, w_br_a, w_br_b, w_out, g_pre_mix, g_post_mix,
                  g_pre_mlp, g_post_mlp, w_up, w_down, g_pre_ple, g_post_ple, w_ple, w_ple_gate):
    d = w_in.shape[0]
    mix = ml_norm.shape[0]
    H = ML_HEADS
    o_if = 4 * mix
    o_sb = o_if + 2 * H
    o_gate = o_sb + 3 * mix
    w_main = jnp.concatenate([w_in[:, :2 * mix], w_in[:, o_gate:], w_in[:, 2 * mix:o_if],
                              w_in[:, o_sb:o_gate]], axis=1).astype(BF16)
    zpad = jnp.zeros((d, LANES - H), w_in.dtype)
    w_if = jnp.concatenate([w_in[:, o_if:o_if + H], zpad, w_in[:, o_if + H:o_sb], zpad],
                           axis=1).astype(BF16)
    bpad = jnp.zeros((LANES - H,), F32)
    bif = jnp.concatenate([b_if[:H], bpad, b_if[H:], bpad])[None, :]
    row = lambda a: a[None, :].astype(F32)
    return dict(w_main=w_main, w_if=w_if, bif=bif, cw=conv_w.astype(F32), cb=row(conv_b),
                mln=row(ml_norm), wa=w_br_a.astype(BF16), wb=w_br_b.astype(BF16),
                wo=w_out.astype(BF16), g_pre_mix=row(g_pre_mix), g_post_mix=row(g_post_mix),
                g_pre_mlp=row(g_pre_mlp), g_post_mlp=row(g_post_mlp), wu=w_up.astype(BF16),
                wd=w_down.astype(BF16), g_pre_ple=row(g_pre_ple), g_post_ple=row(g_post_ple),
                wp=w_ple.astype(BF16), wg=w_ple_gate.astype(BF16))


def _layer(x, p, conv_buf, c0, n0, m0, kv_past, W):
    bsz, t, d = x.shape
    m_rows = bsz * t
    mix = W["mln"].shape[1]
    d_head = mix // SB_HEADS
    x2 = x.reshape(m_rows, d)
    proj, qkv, k_s, v_s, gates = _in_proj(x2, W["g_pre_mix"], W["w_main"], W["w_if"], mix,
                                          d_head ** -0.5 * LOG2E)
    p3 = proj.reshape(bsz, t, -1)
    qkv3 = qkv.reshape(bsz, t, -1)
    if3 = gates.reshape(bsz, t, GATE_W)

    cst8 = jnp.pad(conv_buf.astype(F32), ((0, 0), (SUBLANES - (CONV_W - 1), 0), (0, 0)))
    m0p = jnp.pad(m0.astype(F32), ((0, 0), (0, LANES - ML_HEADS)))[:, None, :]
    L = min(t, ML_CHUNK)
    h_m, c_new, n_new, m_new, conv8 = _mlstm(p3, if3, cst8, c0.astype(F32), n0.astype(F32), m0p,
                                             W["cw"], W["cb"], W["bif"], W["mln"], L)
    if kv_past is None:
        h_s = _sb_prompt(qkv3, d_head)
    else:
        k_past, v_past = kv_past
        past = k_past.shape[1]
        h_s = _sb_sample(qkv3, k_past.reshape(bsz, past * SB_HEADS, d_head),
                         v_past.reshape(bsz, past * SB_HEADS, d_head), d_head)

    x1 = _mix_out(x2, h_m.reshape(m_rows, mix), h_s.reshape(m_rows, mix), proj,
                  W["wa"], W["wb"], W["wo"], W["g_post_mix"])
    x3 = _mlp(x1, W["g_pre_mlp"], W["wu"], W["wd"], W["g_post_mlp"])
    y = _ple(x3, p.reshape(m_rows, -1), W["g_pre_ple"], W["wg"], W["wp"], W["g_post_ple"])

    k_s = k_s.reshape(bsz, t, SB_HEADS, d_head)
    v_s = v_s.reshape(bsz, t, SB_HEADS, d_head)
    state = (k_s, v_s, conv8[:, SUBLANES - (CONV_W - 1):, :], c_new, n_new,
             m_new[:, 0, :ML_HEADS])
    return y.reshape(bsz, t, d), state


def kernel(x_prompt, x_sample, p_prompt, p_sample, cache_sb_k, cache_sb_v, state_conv,
           state_mlstm_C, state_mlstm_n, state_mlstm_m, w_in, b_if, conv_w, conv_b, ml_norm,
           w_br_a, w_br_b, w_out, g_pre_mix, g_post_mix, g_pre_mlp, g_post_mlp, w_up, w_down,
           g_pre_ple, g_post_ple, w_ple, w_ple_gate):
    depth = w_in.shape[0]
    bp = x_prompt.shape[0]
    mix = ml_norm.shape[1]
    dh = mix // ML_HEADS
    yp, ys = x_prompt, x_sample
    st_p, st_s = [], []
    for i in range(depth):
        W = _prep_weights(w_in[i], b_if[i], conv_w[i], conv_b[i], ml_norm[i], w_br_a[i],
                          w_br_b[i], w_out[i], g_pre_mix[i], g_post_mix[i], g_pre_mlp[i],
                          g_post_mlp[i], w_up[i], w_down[i], g_pre_ple[i], g_post_ple[i],
                          w_ple[i], w_ple_gate[i])
        yp, sp = _layer(yp, p_prompt[i],
                        jnp.zeros((bp, CONV_W - 1, 2 * mix), F32),
                        jnp.zeros((bp, ML_HEADS, dh, dh), F32),
                        jnp.zeros((bp, ML_HEADS, dh), F32),
                        jnp.zeros((bp, ML_HEADS), F32),
                        None, W)
        ys, ss = _layer(ys, p_sample[i], state_conv[i], state_mlstm_C[i], state_mlstm_n[i],
                        state_mlstm_m[i], (cache_sb_k[i], cache_sb_v[i]), W)
        st_p.append(sp)
        st_s.append(ss)
    stk = lambda sts, j: jnp.stack([s[j] for s in sts])
    return (yp, ys,
            stk(st_p, 0), stk(st_p, 1), stk(st_p, 2), stk(st_p, 3), stk(st_p, 4), stk(st_p, 5),
            stk(st_s, 0), stk(st_s, 1), stk(st_s, 2), stk(st_s, 3), stk(st_s, 4), stk(st_s, 5))
```

```python
import functools

import jax
import jax.numpy as jnp
from jax import lax
from jax.experimental import pallas as pl
from jax.experimental.pallas import tpu as pltpu

F32 = jnp.float32
BF16 = jnp.bfloat16
EPS = 1e-6
LOG2E = 1.4426950408889634

LANES = 128
SUBLANES = 8
VMEM_LIMIT = 56 * 1024 * 1024

ML_HEADS = 4
SB_HEADS = 8
CONV_W = 4
ML_CHUNK = 256
SB_TILE = 256
GATE_W = 2 * LANES
SB_DEAD_CARRY = 160.0

NT_DIMS = (((1,), (1,)), ((), ()))


def _rms(x, g):
    return x * lax.rsqrt(jnp.mean(x * x, axis=-1, keepdims=True) + EPS) * g


def _sigmoid(x):
    return 1.0 / (1.0 + jnp.exp(-x))


def _softplus(z):
    return jnp.maximum(z, 0.0) + jnp.log(1.0 + jnp.exp(-jnp.abs(z)))


def _softplus2(z):
    neg_abs = lax.bitcast_convert_type(
        lax.bitcast_convert_type(z, jnp.uint32) | jnp.uint32(0x80000000), F32)
    return jnp.maximum(z, 0.0) + jnp.log(1.0 + jnp.exp2(neg_abs)) * LOG2E


def _pad_rows(x, rows):
    if x.shape[0] == rows:
        return x
    return jnp.concatenate([x, jnp.zeros((rows - x.shape[0],) + x.shape[1:], x.dtype)], axis=0)


def _dot(a, b):
    return jnp.dot(a, b, preferred_element_type=F32)


def _params(sem):
    return pltpu.CompilerParams(dimension_semantics=sem, vmem_limit_bytes=VMEM_LIMIT)


def _resident(shape):
    return pl.BlockSpec(shape, lambda *_: (0,) * len(shape), pipeline_mode=pl.Buffered(1))


def _in_proj_kernel(x_ref, g_ref, w_ref, wif_ref, p_ref, qkv_ref, ks_ref, vs_ref, if_ref, h_scr,
                    *, n_main, r, q_scale):
    j = pl.program_id(1)

    @pl.when(j == 0)
    def _():
        h = _rms(x_ref[...], g_ref[...]).astype(BF16)
        h_scr[...] = h
        if_ref[...] = _dot(h, wif_ref[...])

    def tile():
        return _dot(h_scr[...], w_ref[...])

    @pl.when(j < n_main)
    def _():
        p_ref[...] = tile()

    @pl.when((j >= n_main) & (j < n_main + r))
    def _():
        qkv_ref[...] = (tile() * q_scale).astype(BF16)

    @pl.when((j >= n_main + r) & (j < n_main + 2 * r))
    def _():
        ks_ref[...] = tile()
        qkv_ref[...] = ks_ref[...].astype(BF16)

    @pl.when(j >= n_main + 2 * r)
    def _():
        vs_ref[...] = tile()
        qkv_ref[...] = vs_ref[...].astype(BF16)


def _in_proj(x2, g, w_main, w_if, mix, q_scale, tm=1024, tn=1024):
    m, d = x2.shape
    assert m % tm == 0 and mix % tn == 0 and w_main.shape[1] % mix == 0
    r = mix // tn
    n_tiles = w_main.shape[1] // tn
    n_main = n_tiles - 3 * r
    kern = functools.partial(_in_proj_kernel, n_main=n_main, r=r, q_scale=q_scale)
    return pl.pallas_call(
        kern,
        grid=(m // tm, n_tiles),
        in_specs=[pl.BlockSpec((tm, d), lambda i, j: (i, 0), pipeline_mode=pl.Buffered(1)),
                  pl.BlockSpec((1, d), lambda i, j: (0, 0)),
                  pl.BlockSpec((d, tn), lambda i, j: (0, j)),
                  pl.BlockSpec((d, GATE_W), lambda i, j: (0, 0), pipeline_mode=pl.Buffered(1))],
        out_specs=[pl.BlockSpec((tm, tn), lambda i, j: (i, jnp.minimum(j, n_main - 1))),
                   pl.BlockSpec((tm, tn), lambda i, j: (i, jnp.maximum(j - n_main, 0))),
                   pl.BlockSpec((tm, tn), lambda i, j: (i, jnp.clip(j - n_main - r, 0, r - 1))),
                   pl.BlockSpec((tm, tn), lambda i, j: (i, jnp.clip(j - n_main - 2 * r, 0, r - 1))),
                   pl.BlockSpec((tm, GATE_W), lambda i, j: (i, 0))],
        out_shape=[jax.ShapeDtypeStruct((m, n_main * tn), F32),
                   jax.ShapeDtypeStruct((m, 3 * mix), BF16),
                   jax.ShapeDtypeStruct((m, mix), F32),
                   jax.ShapeDtypeStruct((m, mix), F32),
                   jax.ShapeDtypeStruct((m, GATE_W), F32)],
        scratch_shapes=[pltpu.VMEM((tm, d), BF16)],
        compiler_params=_params(("parallel", "arbitrary")),
        name="in_proj",
    )(x2, g, w_main, w_if)


def _split3(x):
    hi = x.astype(BF16)
    r = x - hi.astype(F32)
    mid = r.astype(BF16)
    lo = (r - mid.astype(F32)).astype(BF16)
    return hi, mid, lo


def _mlstm_kernel(qk_ref, v_ref, o_ref, if_ref, cst_ref, c0_ref, n0_ref, m0_ref,
                  cw_ref, cb_ref, bif_ref, mln_ref, tri_ref,
                  h_ref, cout_ref, nout_ref, mout_ref, cnew_ref,
                  ext_scr, c_scr, n_scr, m_scr, *, L, Lp, H, dh):
    c = pl.program_id(1)
    mix = H * dh

    @pl.when(c == 0)
    def _():
        ext_scr[0:SUBLANES, :] = cst_ref[0]
        c_scr[...] = c0_ref[0]
        n_scr[...] = n0_ref[0]
        m_scr[...] = m0_ref[0]

    u = qk_ref[0]
    ext_scr[SUBLANES:SUBLANES + L, :] = u
    conv = cb_ref[...] + cw_ref[CONV_W - 1:CONV_W, :] * u
    for j in range(CONV_W - 1):
        lo = SUBLANES - (CONV_W - 1) + j
        conv = conv + cw_ref[j:j + 1, :] * ext_scr[lo:lo + L, :]
    tail = ext_scr[L:L + SUBLANES, :]
    ext_scr[0:SUBLANES, :] = tail
    cnew_ref[0] = tail
    qk = conv * _sigmoid(conv)
    q_all = qk[:, :mix]
    k_all = qk[:, mix:] * (dh ** -0.5)

    ifp = if_ref[0] + bif_ref[...]
    ig = ifp[:, 0:LANES]
    lf = -_softplus(-ifp[:, LANES:2 * LANES])
    tri = tri_ref[...]
    hi, mid, lo3 = _split3(_pad_rows(lf, Lp))
    b = _dot(tri, hi) + _dot(tri, mid) + _dot(tri, lo3)
    g = ig - b
    g_t = _pad_rows(g, Lp).T
    m_old = m_scr[...]
    m_last = jnp.maximum(m_old, jnp.max(g, axis=0, keepdims=True))
    cdec = jnp.exp(m_old - m_last)
    m_scr[...] = b[L - 1:L, :] + m_last
    wk = jnp.exp(g - m_last)

    row = lax.broadcasted_iota(jnp.int32, (L, Lp), 0)
    col = lax.broadcasted_iota(jnp.int32, (L, Lp), 1)
    causal = col <= row

    for h in range(H):
        sl = slice(h * dh, (h + 1) * dh)
        qh = q_all[:, sl]
        kh = k_all[:, sl]
        qb = qh.astype(BF16)
        kb = _pad_rows(kh, Lp).astype(BF16)
        vb = _pad_rows(v_ref[0, :, sl], Lp).astype(BF16)
        m_h = m_old[:, h:h + 1]
        dm = jnp.where(causal, g_t[h:h + 1, :], -jnp.inf)
        m_col = jnp.maximum(jnp.max(dm, axis=1, keepdims=True), m_h)
        s = lax.dot_general(qb, kb, NT_DIMS, preferred_element_type=F32) * jnp.exp(dm - m_col)
        dec = jnp.exp(m_h - m_col)
        num = _dot(s.astype(BF16), vb) + dec * _dot(qb, c_scr[h].astype(BF16))
        qn = jnp.sum(qh * n_scr[h:h + 1, :], axis=1, keepdims=True)
        den = jnp.sum(s, axis=1, keepdims=True) + dec * qn
        m_t = b[:, h:h + 1] + m_col
        hh = num / jnp.maximum(jnp.abs(den), jnp.exp(-m_t))
        hm = _sigmoid(o_ref[0, :, sl]) * hh
        h_ref[0, :, sl] = _rms(hm, mln_ref[:, sl]).astype(h_ref.dtype)
        kw = kh * wk[:, h:h + 1]
        cd_h = cdec[:, h:h + 1]
        c_scr[h] = cd_h * c_scr[h] + _dot(_pad_rows(kw, Lp).T.astype(BF16), vb)
        n_scr[h:h + 1, :] = cd_h * n_scr[h:h + 1, :] + jnp.sum(kw, axis=0, keepdims=True)

    @pl.when(c == pl.num_programs(1) - 1)
    def _():
        cout_ref[0] = c_scr[...]
        nout_ref[0] = n_scr[...]
        mout_ref[0] = m_scr[...]


def _mlstm(p3, if3, cst8, c0, n0, m0p, cw, cb, bif, mln, L):
    bsz, t, _ = p3.shape
    H = ML_HEADS
    mix = mln.shape[1]
    dh = mix // H
    assert t % L == 0 and L % SUBLANES == 0
    nc = t // L
    Lp = max(L, LANES)
    tri = (jnp.arange(Lp)[None, :] <= jnp.arange(L)[:, None]).astype(BF16)
    kern = functools.partial(_mlstm_kernel, L=L, Lp=Lp, H=H, dh=dh)
    return pl.pallas_call(
        kern,
        grid=(bsz, nc),
        in_specs=[pl.BlockSpec((1, L, 2 * mix), lambda b, c: (b, c, 0)),
                  pl.BlockSpec((1, L, mix), lambda b, c: (b, c, 6)),
                  pl.BlockSpec((1, L, mix), lambda b, c: (b, c, 7)),
                  pl.BlockSpec((1, L, GATE_W), lambda b, c: (b, c, 0)),
                  pl.BlockSpec((1, SUBLANES, 2 * mix), lambda b, c: (b, 0, 0)),
                  pl.BlockSpec((1, H, dh, dh), lambda b, c: (b, 0, 0, 0)),
                  pl.BlockSpec((1, H, dh), lambda b, c: (b, 0, 0)),
                  pl.BlockSpec((1, 1, LANES), lambda b, c: (b, 0, 0)),
                  pl.BlockSpec((CONV_W, 2 * mix), lambda b, c: (0, 0)),
                  pl.BlockSpec((1, 2 * mix), lambda b, c: (0, 0)),
                  pl.BlockSpec((1, GATE_W), lambda b, c: (0, 0)),
                  pl.BlockSpec((1, mix), lambda b, c: (0, 0)),
                  pl.BlockSpec((L, Lp), lambda b, c: (0, 0))],
        out_specs=[pl.BlockSpec((1, L, mix), lambda b, c: (b, c, 0)),
                   pl.BlockSpec((1, H, dh, dh), lambda b, c: (b, 0, 0, 0)),
                   pl.BlockSpec((1, H, dh), lambda b, c: (b, 0, 0)),
                   pl.BlockSpec((1, 1, LANES), lambda b, c: (b, 0, 0)),
                   pl.BlockSpec((1, SUBLANES, 2 * mix), lambda b, c: (b, 0, 0))],
        out_shape=[jax.ShapeDtypeStruct((bsz, t, mix), BF16),
                   jax.ShapeDtypeStruct((bsz, H, dh, dh), F32),
                   jax.ShapeDtypeStruct((bsz, H, dh), F32),
                   jax.ShapeDtypeStruct((bsz, 1, LANES), F32),
                   jax.ShapeDtypeStruct((bsz, SUBLANES, 2 * mix), F32)],
        scratch_shapes=[pltpu.VMEM((L + SUBLANES, 2 * mix), F32),
                        pltpu.VMEM((H, dh, dh), F32),
                        pltpu.VMEM((H, dh), F32),
                        pltpu.VMEM((1, LANES), F32)],
        compiler_params=_params(("parallel", "arbitrary")),
        name="mlstm",
    )(p3, p3, p3, if3, cst8, c0, n0, m0p, cw, cb, bif, mln, tri)


def _sb_cumsum(z, u_mat, mask):
    sp = _softplus2(z)
    if mask is not None:
        sp = jnp.where(mask, sp, 0.0)
    hi = sp.astype(BF16)
    lo = (sp - hi.astype(F32)).astype(BF16)
    return _dot(hi, u_mat) + _dot(lo, u_mat)


def _sb_weights(z, r, carry, mask):
    a = jnp.exp2(z - r - carry)
    if mask is not None:
        a = jnp.where(mask, a, 0.0)
    return a.astype(BF16), carry + r[:, 0:1]


def _sb_blocks(qs, kblks, vblks, u_mat, carries, mask, group):
    n = len(qs)
    zs, rs, out = [None] * n, [None] * n, [None] * n

    def logits(h):
        zs[h] = lax.dot_general(qs[h], kblks[h], NT_DIMS, preferred_element_type=F32)

    def cumsum(h):
        rs[h] = _sb_cumsum(zs[h], u_mat, mask)

    def weights(h):
        a, car = _sb_weights(zs[h], rs[h], carries[h], mask)
        out[h] = (_dot(a, vblks[h]), car)

    stages = (logits, cumsum, weights)
    groups = [range(g, min(g + group, n)) for g in range(0, n, group)]
    for step in range(len(groups) + len(stages) - 1):
        for k, stage in enumerate(stages):
            if 0 <= step - k < len(groups):
                for h in groups[step - k]:
                    stage(h)
    return out


def _strict_causal(tq, sk):
    row = lax.broadcasted_iota(jnp.int32, (tq, sk), 0)
    col = lax.broadcasted_iota(jnp.int32, (tq, sk), 1)
    return col < row


def _sb_prompt_kernel(q_ref, k_ref, v_ref, u_ref, o_ref, z_scr, a_scr, acc_scr, car_scr,
                      *, tq, dh, hpb):
    qi = pl.program_id(2)
    u_mat = u_ref[...]
    heads = [slice(h * dh, (h + 1) * dh) for h in range(hpb)]
    qs = [q_ref[0, :, sl] for sl in heads]

    def rows(j):
        return pl.ds(pl.multiple_of(jnp.maximum(qi - j, 0) * tq, tq), tq)

    def logits(j):
        r = rows(j)
        return [lax.dot_general(qs[h], k_ref[0, r, sl], NT_DIMS, preferred_element_type=F32)
                for h, sl in enumerate(heads)]

    def weights(zs, mask):
        rs = [None] * hpb

        def cumsum(h):
            rs[h] = _sb_cumsum(zs[h], u_mat, mask)

        def finish(h):
            a_scr[h], car_scr[h] = _sb_weights(zs[h], rs[h], car_scr[h], mask)

        lag = min(1, hpb - 1)
        for step in range(hpb + lag):
            if step < hpb:
                cumsum(step)
            if step >= lag:
                finish(step - lag)

    z0 = logits(0)
    z1 = logits(1)
    for h in range(hpb):
        car_scr[h] = jnp.zeros((tq, 1), F32)
        acc_scr[h] = jnp.zeros((tq, dh), F32)
    weights(z0, _strict_causal(tq, tq))
    for h in range(hpb):
        z_scr[h] = z1[h]

    def apply_weights(i):
        r = rows(i)
        for h, sl in enumerate(heads):
            acc_scr[h] += _dot(a_scr[h], v_ref[0, r, sl])

    def step(i):
        apply_weights(i)
        zn = logits(i + 2)
        for h in range(hpb):
            z_scr[hpb + h] = zn[h]
        weights(z_scr, None)
        for h in range(hpb):
            z_scr[h] = z_scr[hpb + h]

    def min_carry():
        m = car_scr[0]
        for h in range(1, hpb):
            m = jnp.minimum(m, car_scr[h])
        return jnp.min(m)

    def more(state):
        i, mc = state
        return (i < qi) & (mc < SB_DEAD_CARRY)

    def body(state):
        i, _ = state
        step(i)
        return i + 1, min_carry()

    last, _ = lax.while_loop(more, body, (jnp.int32(0), min_carry()))
    apply_weights(last)

    for h, sl in enumerate(heads):
        o_ref[0, :, sl] = acc_scr[h].astype(o_ref.dtype)


def _sb_prompt(qkv3, d_head, tq=SB_TILE, hpb=4):
    bsz, t, _ = qkv3.shape
    H = SB_HEADS
    assert t % tq == 0 and H % hpb == 0
    ng = H // hpb
    w = hpb * d_head
    u_mat = (jnp.arange(tq)[:, None] >= jnp.arange(tq)[None, :]).astype(BF16)
    kern = functools.partial(_sb_prompt_kernel, tq=tq, dh=d_head, hpb=hpb)
    return pl.pallas_call(
        kern,
        grid=(bsz, ng, t // tq),
        in_specs=[pl.BlockSpec((1, tq, w), lambda b, g, i: (b, i, g)),
                  pl.BlockSpec((1, t, w), lambda b, g, i: (b, 0, ng + g)),
                  pl.BlockSpec((1, t, w), lambda b, g, i: (b, 0, 2 * ng + g)),
                  pl.BlockSpec((tq, tq), lambda b, g, i: (0, 0))],
        out_specs=pl.BlockSpec((1, tq, w), lambda b, g, i: (b, i, g)),
        out_shape=jax.ShapeDtypeStruct((bsz, t, H * d_head), BF16),
        scratch_shapes=[pltpu.VMEM((2 * hpb, tq, tq), F32),
                        pltpu.VMEM((hpb, tq, tq), BF16),
                        pltpu.VMEM((hpb, tq, d_head), F32),
                        pltpu.VMEM((hpb, tq, 1), F32)],
        compiler_params=_params(("parallel", "parallel", "arbitrary")),
        name="sb_prompt",
    )(qkv3, qkv3, qkv3, u_mat)


def _sb_sample_kernel(q_ref, kn_ref, vn_ref, ck_ref, cv_ref, un_ref, up_ref, o_ref, acc_scr, car_scr,
                      *, L, Lp, H, dh, sk, nsub, group):
    c = pl.program_id(1)
    heads = [slice(h * dh, (h + 1) * dh) for h in range(H)]
    qs = [q_ref[0, :, sl] for sl in heads]

    @pl.when(c == 0)
    def _():
        mask = _strict_causal(L, Lp)
        res = _sb_blocks(qs, [_pad_rows(kn_ref[0, :, sl], Lp) for sl in heads],
                         [_pad_rows(vn_ref[0, :, sl], Lp) for sl in heads], un_ref[...],
                         [jnp.zeros((L, 1), F32)] * H, mask, group)
        for h, (pv, car) in enumerate(res):
            acc_scr[h] = pv
            car_scr[h] = car

    u_mat = up_ref[...]

    def min_carry():
        m = car_scr[0]
        for h in range(1, H):
            m = jnp.minimum(m, car_scr[h])
        return jnp.min(m)

    def more(state):
        i, mc = state
        return (i < nsub) & (mc < SB_DEAD_CARRY)

    def body(state):
        i, _ = state
        off = pl.multiple_of((nsub - 1 - i) * sk * H, sk * H)
        res2 = _sb_blocks(qs,
                          [ck_ref[0, pl.ds(off + h, sk, stride=H), :].astype(BF16) for h in range(H)],
                          [cv_ref[0, pl.ds(off + h, sk, stride=H), :].astype(BF16) for h in range(H)],
                          u_mat, [car_scr[h] for h in range(H)], None, group)
        for h, (pv2, car2) in enumerate(res2):
            acc_scr[h] += pv2
            car_scr[h] = car2
        return i + 1, min_carry()

    lax.while_loop(more, body, (jnp.int32(0), min_carry()))

    @pl.when(c == pl.num_programs(1) - 1)
    def _():
        for h, sl in enumerate(heads):
            o_ref[0, :, sl] = acc_scr[h].astype(o_ref.dtype)


def _sb_sample(qkv3, ck, cv, d_head, sk=SB_TILE, chunk=2048, group=4):
    bsz, L, _ = qkv3.shape
    H = SB_HEADS
    past = ck.shape[1] // H
    chunk = min(chunk, past)
    assert past % chunk == 0 and chunk % sk == 0
    w = H * d_head
    nch = past // chunk
    Lp = max(L, LANES)
    u_new = (jnp.arange(Lp)[:, None] >= jnp.arange(Lp)[None, :]).astype(BF16)
    u_past = (jnp.arange(sk)[:, None] >= jnp.arange(sk)[None, :]).astype(BF16)
    kern = functools.partial(_sb_sample_kernel, L=L, Lp=Lp, H=H, dh=d_head, sk=sk,
                             nsub=chunk // sk, group=group)
    return pl.pallas_call(
        kern,
        grid=(bsz, nch),
        in_specs=[pl.BlockSpec((1, L, w), lambda b, c: (b, 0, 0)),
                  pl.BlockSpec((1, L, w), lambda b, c: (b, 0, 1)),
                  pl.BlockSpec((1, L, w), lambda b, c: (b, 0, 2)),
                  pl.BlockSpec((1, chunk * H, d_head), lambda b, c: (b, nch - 1 - c, 0)),
                  pl.BlockSpec((1, chunk * H, d_head), lambda b, c: (b, nch - 1 - c, 0)),
                  pl.BlockSpec((Lp, Lp), lambda b, c: (0, 0)),
                  pl.BlockSpec((sk, sk), lambda b, c: (0, 0))],
        out_specs=pl.BlockSpec((1, L, w), lambda b, c: (b, 0, 0)),
        out_shape=jax.ShapeDtypeStruct((bsz, L, w), BF16),
        scratch_shapes=[pltpu.VMEM((H, L, d_head), F32),
                        pltpu.VMEM((H, L, 1), F32)],
        compiler_params=_params(("parallel", "arbitrary")),
        name="sb_sample",
    )(qkv3, qkv3, qkv3, ck, cv, u_new, u_past)


def _mix_out_kernel(x_ref, hm_ref, hs_ref, ga_ref, gb_ref, wa_ref, wb_ref, wo_ref, g_ref, o_ref):
    ua = _dot(hm_ref[...], wa_ref[...])
    ub = _dot(hs_ref[...], wb_ref[...])
    u = _sigmoid(ga_ref[...]) * ua + _sigmoid(gb_ref[...]) * ub
    y = _dot(u.astype(BF16), wo_ref[...])
    o_ref[...] = x_ref[...] + _rms(y, g_ref[...])


def _mix_out(x2, hm, hs, p2, wa, wb, wo, g, tm=256):
    m, d = x2.shape
    mix = hm.shape[1]
    assert m % tm == 0 and d == 2 * mix
    return pl.pallas_call(
        _mix_out_kernel,
        grid=(m // tm,),
        in_specs=[pl.BlockSpec((tm, d), lambda i: (i, 0)),
                  pl.BlockSpec((tm, mix), lambda i: (i, 0)),
                  pl.BlockSpec((tm, mix), lambda i: (i, 0)),
                  pl.BlockSpec((tm, d), lambda i: (i, 1)),
                  pl.BlockSpec((tm, d), lambda i: (i, 2)),
                  _resident((mix, d)), _resident((mix, d)), _resident((d, d)),
                  _resident((1, d))],
        out_specs=pl.BlockSpec((tm, d), lambda i: (i, 0)),
        out_shape=jax.ShapeDtypeStruct((m, d), F32),
        compiler_params=_params(("parallel",)),
        name="mix_out",
    )(x2, hm, hs, p2, p2, wa, wb, wo, g)


def _mlp_kernel(x_ref, g1_ref, wu_ref, wd_ref, g2_ref, o_ref, h_scr, acc_scr):
    j = pl.program_id(1)

    @pl.when(j == 0)
    def _():
        h_scr[...] = _rms(x_ref[...], g1_ref[...]).astype(BF16)

    a = _dot(h_scr[...], wu_ref[...])
    a = jnp.square(jnp.maximum(a, 0.0)).astype(BF16)

    @pl.when(j == 0)
    def _():
        acc_scr[...] = _dot(a, wd_ref[...])

    @pl.when(j > 0)
    def _():
        acc_scr[...] += _dot(a, wd_ref[...])

    @pl.when(j == pl.num_programs(1) - 1)
    def _():
        o_ref[...] = x_ref[...] + _rms(acc_scr[...], g2_ref[...])


def _mlp(x2, g1, wu, wd, g2, tm=512, tf=1024):
    m, d = x2.shape
    dff = wu.shape[1]
    assert m % tm == 0 and dff % tf == 0
    return pl.pallas_call(
        _mlp_kernel,
        grid=(m // tm, dff // tf),
        in_specs=[pl.BlockSpec((tm, d), lambda i, j: (i, 0)),
                  pl.BlockSpec((1, d), lambda i, j: (0, 0)),
                  pl.BlockSpec((d, tf), lambda i, j: (0, j)),
                  pl.BlockSpec((tf, d), lambda i, j: (j, 0)),
                  pl.BlockSpec((1, d), lambda i, j: (0, 0))],
        out_specs=pl.BlockSpec((tm, d), lambda i, j: (i, 0)),
        out_shape=jax.ShapeDtypeStruct((m, d), F32),
        scratch_shapes=[pltpu.VMEM((tm, d), BF16), pltpu.VMEM((tm, d), F32)],
        compiler_params=_params(("parallel", "arbitrary")),
        name="mlp",
    )(x2, g1, wu, wd, g2)


def _ple_kernel(x_ref, p_ref, g1_ref, wg_ref, wp_ref, g2_ref, o_ref):
    x = x_ref[...]
    gate = _sigmoid(_dot(_rms(x, g1_ref[...]).astype(BF16), wg_ref[...]))
    ple = _dot(p_ref[...].astype(BF16), wp_ref[...]) * gate
    o_ref[...] = x + _rms(ple, g2_ref[...])


def _ple(x2, p2, g1, wg, wp, g2, tm=512):
    m, d = x2.shape
    pd = p2.shape[1]
    assert m % tm == 0
    return pl.pallas_call(
        _ple_kernel,
        grid=(m // tm,),
        in_specs=[pl.BlockSpec((tm, d), lambda i: (i, 0)),
                  pl.BlockSpec((tm, pd), lambda i: (i, 0)),
                  _resident((1, d)), _resident((d, d)), _resident((pd, d)), _resident((1, d))],
        out_specs=pl.BlockSpec((tm, d), lambda i: (i, 0)),
        out_shape=jax.ShapeDtypeStruct((m, d), F32),
        compiler_params=_params(("parallel",)),
        name="ple",
    )(x2, p2, g1, wg, wp, g2)


def _prep_weights(w_in, b_if, conv_w, conv_b, ml_norm, w_br_a, w_br_b, w_out, g_pre_mix, g_post_mix,
                  g_pre_mlp, g_post_mlp, w_up, w_down, g_pre_ple, g_post_ple, w_ple, w_ple_gate):
    d = w_in.shape[0]
    mix = ml_norm.shape[0]
    H = ML_HEADS
    o_if = 4 * mix
    o_sb = o_if + 2 * H
    o_gate = o_sb + 3 * mix
    w_main = jnp.concatenate([w_in[:, :2 * mix], w_in[:, o_gate:], w_in[:, 2 * mix:o_if],
                              w_in[:, o_sb:o_gate]], axis=1).astype(BF16)
    zpad = jnp.zeros((d, LANES - H), w_in.dtype)
    w_if = jnp.concatenate([w_in[:, o_if:o_if + H], zpad, w_in[:, o_if + H:o_sb], zpad],
                           axis=1).astype(BF16)
    bpad = jnp.zeros((LANES - H,), F32)
    bif = jnp.concatenate([b_if[:H], bpad, b_if[H:], bpad])[None, :]
    row = lambda a: a[None, :].astype(F32)
    return dict(w_main=w_main, w_if=w_if, bif=bif, cw=conv_w.astype(F32), cb=row(conv_b),
                mln=row(ml_norm), wa=w_br_a.astype(BF16), wb=w_br_b.astype(BF16),
                wo=w_out.astype(BF16), g_pre_mix=row(g_pre_mix), g_post_mix=row(g_post_mix),
                g_pre_mlp=row(g_pre_mlp), g_post_mlp=row(g_post_mlp), wu=w_up.astype(BF16),
                wd=w_down.astype(BF16), g_pre_ple=row(g_pre_ple), g_post_ple=row(g_post_ple),
                wp=w_ple.astype(BF16), wg=w_ple_gate.astype(BF16))


def _layer(x, p, conv_buf, c0, n0, m0, kv_past, W):
    bsz, t, d = x.shape
    m_rows = bsz * t
    mix = W["mln"].shape[1]
    d_head = mix // SB_HEADS
    x2 = x.reshape(m_rows, d)
    proj, qkv, k_s, v_s, gates = _in_proj(x2, W["g_pre_mix"], W["w_main"], W["w_if"], mix,
                                          d_head ** -0.5 * LOG2E)
    p3 = proj.reshape(bsz, t, -1)
    qkv3 = qkv.reshape(bsz, t, -1)
    if3 = gates.reshape(bsz, t, GATE_W)

    cst8 = jnp.pad(conv_buf.astype(F32), ((0, 0), (SUBLANES - (CONV_W - 1), 0), (0, 0)))
    m0p = jnp.pad(m0.astype(F32), ((0, 0), (0, LANES - ML_HEADS)))[:, None, :]
    L = min(t, ML_CHUNK)
    h_m, c_new, n_new, m_new, conv8 = _mlstm(p3, if3, cst8, c0.astype(F32), n0.astype(F32), m0p,
                                             W["cw"], W["cb"], W["bif"], W["mln"], L)
    if kv_past is None:
        h_s = _sb_prompt(qkv3, d_head)
    else:
        k_past, v_past = kv_past
        past = k_past.shape[1]
        h_s = _sb_sample(qkv3, k_past.reshape(bsz, past * SB_HEADS, d_head),
                         v_past.reshape(bsz, past * SB_HEADS, d_head), d_head)

    x1 = _mix_out(x2, h_m.reshape(m_rows, mix), h_s.reshape(m_rows, mix), proj,
                  W["wa"], W["wb"], W["wo"], W["g_post_mix"])
    x3 = _mlp(x1, W["g_pre_mlp"], W["wu"], W["wd"], W["g_post_mlp"])
    y = _ple(x3, p.reshape(m_rows, -1), W["g_pre_ple"], W["wg"], W["wp"], W["g_post_ple"])

    k_s = k_s.reshape(bsz, t, SB_HEADS, d_head)
    v_s = v_s.reshape(bsz, t, SB_HEADS, d_head)
    state = (k_s, v_s, conv8[:, SUBLANES - (CONV_W - 1):, :], c_new, n_new,
             m_new[:, 0, :ML_HEADS])
    return y.reshape(bsz, t, d), state


def kernel(x_prompt, x_sample, p_prompt, p_sample, cache_sb_k, cache_sb_v, state_conv,
           state_mlstm_C, state_mlstm_n, state_mlstm_m, w_in, b_if, conv_w, conv_b, ml_norm,
           w_br_a, w_br_b, w_out, g_pre_mix, g_post_mix, g_pre_mlp, g_post_mlp, w_up, w_down,
           g_pre_ple, g_post_ple, w_ple, w_ple_gate):
    depth = w_in.shape[0]
    bp = x_prompt.shape[0]
    mix = ml_norm.shape[1]
    dh = mix // ML_HEADS
    yp, ys = x_prompt, x_sample
    st_p, st_s = [], []
    for i in range(depth):
        W = _prep_weights(w_in[i], b_if[i], conv_w[i], conv_b[i], ml_norm[i], w_br_a[i],
                          w_br_b[i], w_out[i], g_pre_mix[i], g_post_mix[i], g_pre_mlp[i],
                          g_post_mlp[i], w_up[i], w_down[i], g_pre_ple[i], g_post_ple[i],
                          w_ple[i], w_ple_gate[i])
        yp, sp = _layer(yp, p_prompt[i],
                        jnp.zeros((bp, CONV_W - 1, 2 * mix), F32),
                        jnp.zeros((bp, ML_HEADS, dh, dh), F32),
                        jnp.zeros((bp, ML_HEADS, dh), F32),
                        jnp.zeros((bp, ML_HEADS), F32),
                        None, W)
        ys, ss = _layer(ys, p_sample[i], state_conv[i], state_mlstm_C[i], state_mlstm_n[i],
                        state_mlstm_m[i], (cache_sb_k[i], cache_sb_v[i]), W)
        st_p.append(sp)
        st_s.append(ss)
    stk = lambda sts, j: jnp.stack([s[j] for s in sts])
    return (yp, ys,
            stk(st_p, 0), stk(st_p, 1), stk(st_p, 2), stk(st_p, 3), stk(st_p, 4), stk(st_p, 5),
            stk(st_s, 0), stk(st_s, 1), stk(st_s, 2), stk(st_s, 3), stk(st_s, 4), stk(st_s, 5))
```

```python
import functools

import jax
import jax.numpy as jnp
from jax import lax
from jax.experimental import pallas as pl
from jax.experimental.pallas import tpu as pltpu

F32 = jnp.float32
BF16 = jnp.bfloat16
EPS = 1e-6
LOG2E = 1.4426950408889634

LANES = 128
SUBLANES = 8
VMEM_LIMIT = 56 * 1024 * 1024

ML_HEADS = 4
SB_HEADS = 8
CONV_W = 4
ML_CHUNK = 256
SB_TILE = 256
GATE_W = 2 * LANES
SB_DEAD_CARRY = 160.0

NT_DIMS = (((1,), (1,)), ((), ()))


def _rms(x, g):
    return x * lax.rsqrt(jnp.mean(x * x, axis=-1, keepdims=True) + EPS) * g


def _sigmoid(x):
    return 1.0 / (1.0 + jnp.exp(-x))


def _softplus(z):
    return jnp.maximum(z, 0.0) + jnp.log(1.0 + jnp.exp(-jnp.abs(z)))


def _softplus2(z):
    neg_abs = lax.bitcast_convert_type(
        lax.bitcast_convert_type(z, jnp.uint32) | jnp.uint32(0x80000000), F32)
    return jnp.maximum(z, 0.0) + jnp.log(1.0 + jnp.exp2(neg_abs)) * LOG2E


def _pad_rows(x, rows):
    if x.shape[0] == rows:
        return x
    return jnp.concatenate([x, jnp.zeros((rows - x.shape[0],) + x.shape[1:], x.dtype)], axis=0)


def _dot(a, b):
    return jnp.dot(a, b, preferred_element_type=F32)


def _params(sem):
    return pltpu.CompilerParams(dimension_semantics=sem, vmem_limit_bytes=VMEM_LIMIT)


def _resident(shape):
    return pl.BlockSpec(shape, lambda *_: (0,) * len(shape), pipeline_mode=pl.Buffered(1))


def _in_proj_kernel(x_ref, g_ref, w_ref, wif_ref, p_ref, qkv_ref, ks_ref, vs_ref, if_ref, h_scr,
                    *, n_main, r, q_scale):
    j = pl.program_id(1)

    @pl.when(j == 0)
    def _():
        h = _rms(x_ref[...], g_ref[...]).astype(BF16)
        h_scr[...] = h
        if_ref[...] = _dot(h, wif_ref[...])

    def tile():
        return _dot(h_scr[...], w_ref[...])

    @pl.when(j < n_main)
    def _():
        p_ref[...] = tile()

    @pl.when((j >= n_main) & (j < n_main + r))
    def _():
        qkv_ref[...] = (tile() * q_scale).astype(BF16)

    @pl.when((j >= n_main + r) & (j < n_main + 2 * r))
    def _():
        ks_ref[...] = tile()
        qkv_ref[...] = ks_ref[...].astype(BF16)

    @pl.when(j >= n_main + 2 * r)
    def _():
        vs_ref[...] = tile()
        qkv_ref[...] = vs_ref[...].astype(BF16)


def _in_proj(x2, g, w_main, w_if, mix, q_scale, tm=1024, tn=1024):
    m, d = x2.shape
    assert m % tm == 0 and mix % tn == 0 and w_main.shape[1] % mix == 0
    r = mix // tn
    n_tiles = w_main.shape[1] // tn
    n_main = n_tiles - 3 * r
    kern = functools.partial(_in_proj_kernel, n_main=n_main, r=r, q_scale=q_scale)
    return pl.pallas_call(
        kern,
        grid=(m // tm, n_tiles),
        in_specs=[pl.BlockSpec((tm, d), lambda i, j: (i, 0), pipeline_mode=pl.Buffered(1)),
                  pl.BlockSpec((1, d), lambda i, j: (0, 0)),
                  pl.BlockSpec((d, tn), lambda i, j: (0, j)),
                  pl.BlockSpec((d, GATE_W), lambda i, j: (0, 0), pipeline_mode=pl.Buffered(1))],
        out_specs=[pl.BlockSpec((tm, tn), lambda i, j: (i, jnp.minimum(j, n_main - 1))),
                   pl.BlockSpec((tm, tn), lambda i, j: (i, jnp.maximum(j - n_main, 0))),
                   pl.BlockSpec((tm, tn), lambda i, j: (i, jnp.clip(j - n_main - r, 0, r - 1))),
                   pl.BlockSpec((tm, tn), lambda i, j: (i, jnp.clip(j - n_main - 2 * r, 0, r - 1))),
                   pl.BlockSpec((tm, GATE_W), lambda i, j: (i, 0))],
        out_shape=[jax.ShapeDtypeStruct((m, n_main * tn), F32),
                   jax.ShapeDtypeStruct((m, 3 * mix), BF16),
                   jax.ShapeDtypeStruct((m, mix), F32),
                   jax.ShapeDtypeStruct((m, mix), F32),
                   jax.ShapeDtypeStruct((m, GATE_W), F32)],
        scratch_shapes=[pltpu.VMEM((tm, d), BF16)],
        compiler_params=_params(("parallel", "arbitrary")),
        name="in_proj",
    )(x2, g, w_main, w_if)


def _split3(x):
    hi = x.astype(BF16)
    r = x - hi.astype(F32)
    mid = r.astype(BF16)
    lo = (r - mid.astype(F32)).astype(BF16)
    return hi, mid, lo


def _mlstm_kernel(qk_ref, v_ref, o_ref, if_ref, cst_ref, c0_ref, n0_ref, m0_ref,
                  cw_ref, cb_ref, bif_ref, mln_ref, tri_ref,
                  h_ref, cout_ref, nout_ref, mout_ref, cnew_ref,
                  ext_scr, c_scr, n_scr, m_scr, *, L, Lp, H, dh):
    c = pl.program_id(1)
    mix = H * dh

    @pl.when(c == 0)
    def _():
        ext_scr[0:SUBLANES, :] = cst_ref[0]
        c_scr[...] = c0_ref[0]
        n_scr[...] = n0_ref[0]
        m_scr[...] = m0_ref[0]

    u = qk_ref[0]
    ext_scr[SUBLANES:SUBLANES + L, :] = u
    conv = cb_ref[...] + cw_ref[CONV_W - 1:CONV_W, :] * u
    for j in range(CONV_W - 1):
        lo = SUBLANES - (CONV_W - 1) + j
        conv = conv + cw_ref[j:j + 1, :] * ext_scr[lo:lo + L, :]
    tail = ext_scr[L:L + SUBLANES, :]
    ext_scr[0:SUBLANES, :] = tail
    cnew_ref[0] = tail
    qk = conv * _sigmoid(conv)
    q_all = qk[:, :mix]
    k_all = qk[:, mix:] * (dh ** -0.5)

    ifp = if_ref[0] + bif_ref[...]
    ig = ifp[:, 0:LANES]
    lf = -_softplus(-ifp[:, LANES:2 * LANES])
    tri = tri_ref[...]
    hi, mid, lo3 = _split3(_pad_rows(lf, Lp))
    b = _dot(tri, hi) + _dot(tri, mid) + _dot(tri, lo3)
    g = ig - b
    g_t = _pad_rows(g, Lp).T
    m_old = m_scr[...]
    m_last = jnp.maximum(m_old, jnp.max(g, axis=0, keepdims=True))
    cdec = jnp.exp(m_old - m_last)
    m_scr[...] = b[L - 1:L, :] + m_last
    wk = jnp.exp(g - m_last)

    row = lax.broadcasted_iota(jnp.int32, (L, Lp), 0)
    col = lax.broadcasted_iota(jnp.int32, (L, Lp), 1)
    causal = col <= row

    for h in range(H):
        sl = slice(h * dh, (h + 1) * dh)
        qh = q_all[:, sl]
        kh = k_all[:, sl]
        qb = qh.astype(BF16)
        kb = _pad_rows(kh, Lp).astype(BF16)
        vb = _pad_rows(v_ref[0, :, sl], Lp).astype(BF16)
        m_h = m_old[:, h:h + 1]
        dm = jnp.where(causal, g_t[h:h + 1, :], -jnp.inf)
        m_col = jnp.maximum(jnp.max(dm, axis=1, keepdims=True), m_h)
        s = lax.dot_general(qb, kb, NT_DIMS, preferred_element_type=F32) * jnp.exp(dm - m_col)
        dec = jnp.exp(m_h - m_col)
        num = _dot(s.astype(BF16), vb) + dec * _dot(qb, c_scr[h].astype(BF16))
        qn = jnp.sum(qh * n_scr[h:h + 1, :], axis=1, keepdims=True)
        den = jnp.sum(s, axis=1, keepdims=True) + dec * qn
        m_t = b[:, h:h + 1] + m_col
        hh = num / jnp.maximum(jnp.abs(den), jnp.exp(-m_t))
        hm = _sigmoid(o_ref[0, :, sl]) * hh
        h_ref[0, :, sl] = _rms(hm, mln_ref[:, sl]).astype(h_ref.dtype)
        kw = kh * wk[:, h:h + 1]
        cd_h = cdec[:, h:h + 1]
        c_scr[h] = cd_h * c_scr[h] + _dot(_pad_rows(kw, Lp).T.astype(BF16), vb)
        n_scr[h:h + 1, :] = cd_h * n_scr[h:h + 1, :] + jnp.sum(kw, axis=0, keepdims=True)

    @pl.when(c == pl.num_programs(1) - 1)
    def _():
        cout_ref[0] = c_scr[...]
        nout_ref[0] = n_scr[...]
        mout_ref[0] = m_scr[...]


def _mlstm(p3, if3, cst8, c0, n0, m0p, cw, cb, bif, mln, L):
    bsz, t, _ = p3.shape
    H = ML_HEADS
    mix = mln.shape[1]
    dh = mix // H
    assert t % L == 0 and L % SUBLANES == 0
    nc = t // L
    Lp = max(L, LANES)
    tri = (jnp.arange(Lp)[None, :] <= jnp.arange(L)[:, None]).astype(BF16)
    kern = functools.partial(_mlstm_kernel, L=L, Lp=Lp, H=H, dh=dh)
    return pl.pallas_call(
        kern,
        grid=(bsz, nc),
        in_specs=[pl.BlockSpec((1, L, 2 * mix), lambda b, c: (b, c, 0)),
                  pl.BlockSpec((1, L, mix), lambda b, c: (b, c, 6)),
                  pl.BlockSpec((1, L, mix), lambda b, c: (b, c, 7)),
                  pl.BlockSpec((1, L, GATE_W), lambda b, c: (b, c, 0)),
                  pl.BlockSpec((1, SUBLANES, 2 * mix), lambda b, c: (b, 0, 0)),
                  pl.BlockSpec((1, H, dh, dh), lambda b, c: (b, 0, 0, 0)),
                  pl.BlockSpec((1, H, dh), lambda b, c: (b, 0, 0)),
                  pl.BlockSpec((1, 1, LANES), lambda b, c: (b, 0, 0)),
                  pl.BlockSpec((CONV_W, 2 * mix), lambda b, c: (0, 0)),
                  pl.BlockSpec((1, 2 * mix), lambda b, c: (0, 0)),
                  pl.BlockSpec((1, GATE_W), lambda b, c: (0, 0)),
                  pl.BlockSpec((1, mix), lambda b, c: (0, 0)),
                  pl.BlockSpec((L, Lp), lambda b, c: (0, 0))],
        out_specs=[pl.BlockSpec((1, L, mix), lambda b, c: (b, c, 0)),
                   pl.BlockSpec((1, H, dh, dh), lambda b, c: (b, 0, 0, 0)),
                   pl.BlockSpec((1, H, dh), lambda b, c: (b, 0, 0)),
                   pl.BlockSpec((1, 1, LANES), lambda b, c: (b, 0, 0)),
                   pl.BlockSpec((1, SUBLANES, 2 * mix), lambda b, c: (b, 0, 0))],
        out_shape=[jax.ShapeDtypeStruct((bsz, t, mix), BF16),
                   jax.ShapeDtypeStruct((bsz, H, dh, dh), F32),
                   jax.ShapeDtypeStruct((bsz, H, dh), F32),
                   jax.ShapeDtypeStruct((bsz, 1, LANES), F32),
                   jax.ShapeDtypeStruct((bsz, SUBLANES, 2 * mix), F32)],
        scratch_shapes=[pltpu.VMEM((L + SUBLANES, 2 * mix), F32),
                        pltpu.VMEM((H, dh, dh), F32),
                        pltpu.VMEM((H, dh), F32),
                        pltpu.VMEM((1, LANES), F32)],
        compiler_params=_params(("parallel", "arbitrary")),
        name="mlstm",
    )(p3, p3, p3, if3, cst8, c0, n0, m0p, cw, cb, bif, mln, tri)


def _sb_cumsum(z, u_mat, mask):
    sp = _softplus2(z)
    if mask is not None:
        sp = jnp.where(mask, sp, 0.0)
    hi = sp.astype(BF16)
    lo = (sp - hi.astype(F32)).astype(BF16)
    return _dot(hi, u_mat) + _dot(lo, u_mat)


def _sb_weights(z, r, carry, mask):
    a = jnp.exp2(z - r - carry)
    if mask is not None:
        a = jnp.where(mask, a, 0.0)
    return a.astype(BF16), carry + r[:, 0:1]


def _sb_blocks(qs, kblks, vblks, u_mat, carries, mask, group):
    n = len(qs)
    zs, rs, out = [None] * n, [None] * n, [None] * n

    def logits(h):
        zs[h] = lax.dot_general(qs[h], kblks[h], NT_DIMS, preferred_element_type=F32)

    def cumsum(h):
        rs[h] = _sb_cumsum(zs[h], u_mat, mask)

    def weights(h):
        a, car = _sb_weights(zs[h], rs[h], carries[h], mask)
        out[h] = (_dot(a, vblks[h]), car)

    stages = (logits, cumsum, weights)
    groups = [range(g, min(g + group, n)) for g in range(0, n, group)]
    for step in range(len(groups) + len(stages) - 1):
        for k, stage in enumerate(stages):
            if 0 <= step - k < len(groups):
                for h in groups[step - k]:
                    stage(h)
    return out


def _strict_causal(tq, sk):
    row = lax.broadcasted_iota(jnp.int32, (tq, sk), 0)
    col = lax.broadcasted_iota(jnp.int32, (tq, sk), 1)
    return col < row


def _sb_prompt_kernel(q_ref, k_ref, v_ref, u_ref, o_ref, z_scr, a_scr, acc_scr, car_scr,
                      *, tq, dh, hpb):
    qi = pl.program_id(2)
    u_mat = u_ref[...]
    heads = [slice(h * dh, (h + 1) * dh) for h in range(hpb)]
    qs = [q_ref[0, :, sl] for sl in heads]

    def rows(j):
        return pl.ds(pl.multiple_of(jnp.maximum(qi - j, 0) * tq, tq), tq)

    def logits(j):
        r = rows(j)
        return [lax.dot_general(qs[h], k_ref[0, r, sl], NT_DIMS, preferred_element_type=F32)
                for h, sl in enumerate(heads)]

    def weights(zs, mask):
        rs = [None] * hpb

        def cumsum(h):
            rs[h] = _sb_cumsum(zs[h], u_mat, mask)

        def finish(h):
            a_scr[h], car_scr[h] = _sb_weights(zs[h], rs[h], car_scr[h], mask)

        lag = min(1, hpb - 1)
        for step in range(hpb + lag):
            if step < hpb:
                cumsum(step)
            if step >= lag:
                finish(step - lag)

    z0 = logits(0)
    z1 = logits(1)
    for h in range(hpb):
        car_scr[h] = jnp.zeros((tq, 1), F32)
        acc_scr[h] = jnp.zeros((tq, dh), F32)
    weights(z0, _strict_causal(tq, tq))
    for h in range(hpb):
        z_scr[h] = z1[h]

    def apply_weights(i):
        r = rows(i)
        for h, sl in enumerate(heads):
            acc_scr[h] += _dot(a_scr[h], v_ref[0, r, sl])

    def step(i):
        apply_weights(i)
        zn = logits(i + 2)
        for h in range(hpb):
            z_scr[hpb + h] = zn[h]
        weights(z_scr, None)
        for h in range(hpb):
            z_scr[h] = z_scr[hpb + h]

    def min_carry():
        m = car_scr[0]
        for h in range(1, hpb):
            m = jnp.minimum(m, car_scr[h])
        return jnp.min(m)

    def more(state):
        i, mc = state
        return (i < qi) & (mc < SB_DEAD_CARRY)

    def body(state):
        i, _ = state
        step(i)
        return i + 1, min_carry()

    last, _ = lax.while_loop(more, body, (jnp.int32(0), min_carry()))
    apply_weights(last)

    for h, sl in enumerate(heads):
        o_ref[0, :, sl] = acc_scr[h].astype(o_ref.dtype)


def _sb_prompt(qkv3, d_head, tq=SB_TILE, hpb=4):
    bsz, t, _ = qkv3.shape
    H = SB_HEADS
    assert t % tq == 0 and H % hpb == 0
    ng = H // hpb
    w = hpb * d_head
    u_mat = (jnp.arange(tq)[:, None] >= jnp.arange(tq)[None, :]).astype(BF16)
    kern = functools.partial(_sb_prompt_kernel, tq=tq, dh=d_head, hpb=hpb)
    return pl.pallas_call(
        kern,
        grid=(bsz, ng, t // tq),
        in_specs=[pl.BlockSpec((1, tq, w), lambda b, g, i: (b, i, g)),
                  pl.BlockSpec((1, t, w), lambda b, g, i: (b, 0, ng + g)),
                  pl.BlockSpec((1, t, w), lambda b, g, i: (b, 0, 2 * ng + g)),
                  pl.BlockSpec((tq, tq), lambda b, g, i: (0, 0))],
        out_specs=pl.BlockSpec((1, tq, w), lambda b, g, i: (b, i, g)),
        out_shape=jax.ShapeDtypeStruct((bsz, t, H * d_head), BF16),
        scratch_shapes=[pltpu.VMEM((2 * hpb, tq, tq), F32),
                        pltpu.VMEM((hpb, tq, tq), BF16),
                        pltpu.VMEM((hpb, tq, d_head), F32),
                        pltpu.VMEM((hpb, tq, 1), F32)],
        compiler_params=_params(("parallel", "parallel", "arbitrary")),
        name="sb_prompt",
    )(qkv3, qkv3, qkv3, u_mat)


def _sb_sample_kernel(q_ref, kn_ref, vn_ref, ck_ref, cv_ref, un_ref, up_ref, acc_in, car_in,
                      o_ref, acc_out, car_out, acc_scr, car_scr,
                      *, L, Lp, H, dh, sk, nsub, group, first):
    c = pl.program_id(1)
    heads = [slice(h * dh, (h + 1) * dh) for h in range(H)]
    qs = [q_ref[0, :, sl] for sl in heads]

    @pl.when(c == 0)
    def _():
        if first:
            mask = _strict_causal(L, Lp)
            res = _sb_blocks(qs, [_pad_rows(kn_ref[0, :, sl], Lp) for sl in heads],
                             [_pad_rows(vn_ref[0, :, sl], Lp) for sl in heads], un_ref[...],
                             [jnp.zeros((L, 1), F32)] * H, mask, group)
            for h, (pv, car) in enumerate(res):
                acc_scr[h] = pv
                car_scr[h] = car
        else:
            acc_scr[...] = acc_in[0]
            car_scr[...] = car_in[0]

    u_mat = up_ref[...]

    def min_carry():
        m = car_scr[0]
        for h in range(1, H):
            m = jnp.minimum(m, car_scr[h])
        return jnp.min(m)

    def more(state):
        i, mc = state
        return (i < nsub) & (mc < SB_DEAD_CARRY)

    def body(state):
        i, _ = state
        off = pl.multiple_of((nsub - 1 - i) * sk * H, sk * H)
        res2 = _sb_blocks(qs,
                          [ck_ref[0, pl.ds(off + h, sk, stride=H), :].astype(BF16) for h in range(H)],
                          [cv_ref[0, pl.ds(off + h, sk, stride=H), :].astype(BF16) for h in range(H)],
                          u_mat, [car_scr[h] for h in range(H)], None, group)
        for h, (pv2, car2) in enumerate(res2):
            acc_scr[h] += pv2
            car_scr[h] = car2
        return i + 1, min_carry()

    lax.while_loop(more, body, (jnp.int32(0), min_carry()))

    @pl.when(c == pl.num_programs(1) - 1)
    def _():
        for h, sl in enumerate(heads):
            o_ref[0, :, sl] = acc_scr[h].astype(o_ref.dtype)
        acc_out[0] = acc_scr[...]
        car_out[0] = car_scr[...]


def _sb_sample_call(qkv3, ck, cv, acc, car, d_head, sk, chunk, top, nch, group, first):
    bsz, L, _ = qkv3.shape
    H = SB_HEADS
    w = H * d_head
    Lp = max(L, LANES)
    u_new = (jnp.arange(Lp)[:, None] >= jnp.arange(Lp)[None, :]).astype(BF16)
    u_past = (jnp.arange(sk)[:, None] >= jnp.arange(sk)[None, :]).astype(BF16)
    kern = functools.partial(_sb_sample_kernel, L=L, Lp=Lp, H=H, dh=d_head, sk=sk,
                             nsub=chunk // sk, group=group, first=first)
    return pl.pallas_call(
        kern,
        grid=(bsz, nch),
        in_specs=[pl.BlockSpec((1, L, w), lambda b, c: (b, 0, 0)),
                  pl.BlockSpec((1, L, w), lambda b, c: (b, 0, 1)),
                  pl.BlockSpec((1, L, w), lambda b, c: (b, 0, 2)),
                  pl.BlockSpec((1, chunk * H, d_head), lambda b, c: (b, top - 1 - c, 0)),
                  pl.BlockSpec((1, chunk * H, d_head), lambda b, c: (b, top - 1 - c, 0)),
                  pl.BlockSpec((Lp, Lp), lambda b, c: (0, 0)),
                  pl.BlockSpec((sk, sk), lambda b, c: (0, 0)),
                  pl.BlockSpec((1, H, L, d_head), lambda b, c: (b, 0, 0, 0)),
                  pl.BlockSpec((1, H, L, 1), lambda b, c: (b, 0, 0, 0))],
        out_specs=[pl.BlockSpec((1, L, w), lambda b, c: (b, 0, 0)),
                   pl.BlockSpec((1, H, L, d_head), lambda b, c: (b, 0, 0, 0)),
                   pl.BlockSpec((1, H, L, 1), lambda b, c: (b, 0, 0, 0))],
        out_shape=[jax.ShapeDtypeStruct((bsz, L, w), BF16),
                   jax.ShapeDtypeStruct((bsz, H, L, d_head), F32),
                   jax.ShapeDtypeStruct((bsz, H, L, 1), F32)],
        scratch_shapes=[pltpu.VMEM((H, L, d_head), F32),
                        pltpu.VMEM((H, L, 1), F32)],
        compiler_params=_params(("parallel", "arbitrary")),
        name="sb_sample",
    )(qkv3, qkv3, qkv3, ck, cv, u_new, u_past, acc, car)


def _sb_sample(qkv3, ck, cv, d_head, sk=SB_TILE, chunk=512, group=4):
    bsz, L, _ = qkv3.shape
    H = SB_HEADS
    past = ck.shape[1] // H
    chunk = min(chunk, past)
    assert past % chunk == 0 and chunk % sk == 0
    nch = past // chunk
    acc0 = jnp.zeros((bsz, H, L, d_head), F32)
    car0 = jnp.zeros((bsz, H, L, 1), F32)
    h_new, acc, car = _sb_sample_call(qkv3, ck, cv, acc0, car0, d_head, sk, chunk, nch, 1, group, True)
    if nch == 1:
        return h_new
    return lax.cond(jnp.min(car) >= SB_DEAD_CARRY,
                    lambda: h_new,
                    lambda: _sb_sample_call(qkv3, ck, cv, acc, car, d_head, sk, chunk, nch - 1,
                                            nch - 1, group, False)[0])


def _mix_out_kernel(x_ref, hm_ref, hs_ref, ga_ref, gb_ref, wa_ref, wb_ref, wo_ref, g_ref, o_ref):
    ua = _dot(hm_ref[...], wa_ref[...])
    ub = _dot(hs_ref[...], wb_ref[...])
    u = _sigmoid(ga_ref[...]) * ua + _sigmoid(gb_ref[...]) * ub
    y = _dot(u.astype(BF16), wo_ref[...])
    o_ref[...] = x_ref[...] + _rms(y, g_ref[...])


def _mix_out(x2, hm, hs, p2, wa, wb, wo, g, tm=256):
    m, d = x2.shape
    mix = hm.shape[1]
    assert m % tm == 0 and d == 2 * mix
    return pl.pallas_call(
        _mix_out_kernel,
        grid=(m // tm,),
        in_specs=[pl.BlockSpec((tm, d), lambda i: (i, 0)),
                  pl.BlockSpec((tm, mix), lambda i: (i, 0)),
                  pl.BlockSpec((tm, mix), lambda i: (i, 0)),
                  pl.BlockSpec((tm, d), lambda i: (i, 1)),
                  pl.BlockSpec((tm, d), lambda i: (i, 2)),
                  _resident((mix, d)), _resident((mix, d)), _resident((d, d)),
                  _resident((1, d))],
        out_specs=pl.BlockSpec((tm, d), lambda i: (i, 0)),
        out_shape=jax.ShapeDtypeStruct((m, d), F32),
        compiler_params=_params(("parallel",)),
        name="mix_out",
    )(x2, hm, hs, p2, p2, wa, wb, wo, g)


def _mlp_kernel(x_ref, g1_ref, wu_ref, wd_ref, g2_ref, o_ref, h_scr, acc_scr):
    j = pl.program_id(1)

    @pl.when(j == 0)
    def _():
        h_scr[...] = _rms(x_ref[...], g1_ref[...]).astype(BF16)

    a = _dot(h_scr[...], wu_ref[...])
    a = jnp.square(jnp.maximum(a, 0.0)).astype(BF16)

    @pl.when(j == 0)
    def _():
        acc_scr[...] = _dot(a, wd_ref[...])

    @pl.when(j > 0)
    def _():
        acc_scr[...] += _dot(a, wd_ref[...])

    @pl.when(j == pl.num_programs(1) - 1)
    def _():
        o_ref[...] = x_ref[...] + _rms(acc_scr[...], g2_ref[...])


def _mlp(x2, g1, wu, wd, g2, tm=512, tf=1024):
    m, d = x2.shape
    dff = wu.shape[1]
    assert m % tm == 0 and dff % tf == 0
    return pl.pallas_call(
        _mlp_kernel,
        grid=(m // tm, dff // tf),
        in_specs=[pl.BlockSpec((tm, d), lambda i, j: (i, 0)),
                  pl.BlockSpec((1, d), lambda i, j: (0, 0)),
                  pl.BlockSpec((d, tf), lambda i, j: (0, j)),
                  pl.BlockSpec((tf, d), lambda i, j: (j, 0)),
                  pl.BlockSpec((1, d), lambda i, j: (0, 0))],
        out_specs=pl.BlockSpec((tm, d), lambda i, j: (i, 0)),
        out_shape=jax.ShapeDtypeStruct((m, d), F32),
        scratch_shapes=[pltpu.VMEM((tm, d), BF16), pltpu.VMEM((tm, d), F32)],
        compiler_params=_params(("parallel", "arbitrary")),
        name="mlp",
    )(x2, g1, wu, wd, g2)


def _ple_kernel(x_ref, p_ref, g1_ref, wg_ref, wp_ref, g2_ref, o_ref):
    x = x_ref[...]
    gate = _sigmoid(_dot(_rms(x, g1_ref[...]).astype(BF16), wg_ref[...]))
    ple = _dot(p_ref[...].astype(BF16), wp_ref[...]) * gate
    o_ref[...] = x + _rms(ple, g2_ref[...])


def _ple(x2, p2, g1, wg, wp, g2, tm=512):
    m, d = x2.shape
    pd = p2.shape[1]
    assert m % tm == 0
    return pl.pallas_call(
        _ple_kernel,
        grid=(m // tm,),
        in_specs=[pl.BlockSpec((tm, d), lambda i: (i, 0)),
                  pl.BlockSpec((tm, pd), lambda i: (i, 0)),
                  _resident((1, d)), _resident((d, d)), _resident((pd, d)), _resident((1, d))],
        out_specs=pl.BlockSpec((tm, d), lambda i: (i, 0)),
        out_shape=jax.ShapeDtypeStruct((m, d), F32),
        compiler_params=_params(("parallel",)),
        name="ple",
    )(x2, p2, g1, wg, wp, g2)


def _prep_weights(w_in, b_if, conv_w, conv_b, ml_norm, w_br_a, w_br_b, w_out, g_pre_mix, g_post_mix,
                  g_pre_mlp, g_post_mlp, w_up, w_down, g_pre_ple, g_post_ple, w_ple, w_ple_gate):
    d = w_in.shape[0]
    mix = ml_norm.shape[0]
    H = ML_HEADS
    o_if = 4 * mix
    o_sb = o_if + 2 * H
    o_gate = o_sb + 3 * mix
    w_main = jnp.concatenate([w_in[:, :2 * mix], w_in[:, o_gate:], w_in[:, 2 * mix:o_if],
                              w_in[:, o_sb:o_gate]], axis=1).astype(BF16)
    zpad = jnp.zeros((d, LANES - H), w_in.dtype)
    w_if = jnp.concatenate([w_in[:, o_if:o_if + H], zpad, w_in[:, o_if + H:o_sb], zpad],
                           axis=1).astype(BF16)
    bpad = jnp.zeros((LANES - H,), F32)
    bif = jnp.concatenate([b_if[:H], bpad, b_if[H:], bpad])[None, :]
    row = lambda a: a[None, :].astype(F32)
    return dict(w_main=w_main, w_if=w_if, bif=bif, cw=conv_w.astype(F32), cb=row(conv_b),
                mln=row(ml_norm), wa=w_br_a.astype(BF16), wb=w_br_b.astype(BF16),
                wo=w_out.astype(BF16), g_pre_mix=row(g_pre_mix), g_post_mix=row(g_post_mix),
                g_pre_mlp=row(g_pre_mlp), g_post_mlp=row(g_post_mlp), wu=w_up.astype(BF16),
                wd=w_down.astype(BF16), g_pre_ple=row(g_pre_ple), g_post_ple=row(g_post_ple),
                wp=w_ple.astype(BF16), wg=w_ple_gate.astype(BF16))


def _layer(x, p, conv_buf, c0, n0, m0, kv_past, W):
    bsz, t, d = x.shape
    m_rows = bsz * t
    mix = W["mln"].shape[1]
    d_head = mix // SB_HEADS
    x2 = x.reshape(m_rows, d)
    proj, qkv, k_s, v_s, gates = _in_proj(x2, W["g_pre_mix"], W["w_main"], W["w_if"], mix,
                                          d_head ** -0.5 * LOG2E)
    p3 = proj.reshape(bsz, t, -1)
    qkv3 = qkv.reshape(bsz, t, -1)
    if3 = gates.reshape(bsz, t, GATE_W)

    cst8 = jnp.pad(conv_buf.astype(F32), ((0, 0), (SUBLANES - (CONV_W - 1), 0), (0, 0)))
    m0p = jnp.pad(m0.astype(F32), ((0, 0), (0, LANES - ML_HEADS)))[:, None, :]
    L = min(t, ML_CHUNK)
    h_m, c_new, n_new, m_new, conv8 = _mlstm(p3, if3, cst8, c0.astype(F32), n0.astype(F32), m0p,
                                             W["cw"], W["cb"], W["bif"], W["mln"], L)
    if kv_past is None:
        h_s = _sb_prompt(qkv3, d_head)
    else:
        k_past, v_past = kv_past
        past = k_past.shape[1]
        h_s = _sb_sample(qkv3, k_past.reshape(bsz, past * SB_HEADS, d_head),
                         v_past.reshape(bsz, past * SB_HEADS, d_head), d_head)

    x1 = _mix_out(x2, h_m.reshape(m_rows, mix), h_s.reshape(m_rows, mix), proj,
                  W["wa"], W["wb"], W["wo"], W["g_post_mix"])
    x3 = _mlp(x1, W["g_pre_mlp"], W["wu"], W["wd"], W["g_post_mlp"])
    y = _ple(x3, p.reshape(m_rows, -1), W["g_pre_ple"], W["wg"], W["wp"], W["g_post_ple"])

    k_s = k_s.reshape(bsz, t, SB_HEADS, d_head)
    v_s = v_s.reshape(bsz, t, SB_HEADS, d_head)
    state = (k_s, v_s, conv8[:, SUBLANES - (CONV_W - 1):, :], c_new, n_new,
             m_new[:, 0, :ML_HEADS])
    return y.reshape(bsz, t, d), state


def kernel(x_prompt, x_sample, p_prompt, p_sample, cache_sb_k, cache_sb_v, state_conv,
           state_mlstm_C, state_mlstm_n, state_mlstm_m, w_in, b_if, conv_w, conv_b, ml_norm,
           w_br_a, w_br_b, w_out, g_pre_mix, g_post_mix, g_pre_mlp, g_post_mlp, w_up, w_down,
           g_pre_ple, g_post_ple, w_ple, w_ple_gate):
    depth = w_in.shape[0]
    bp = x_prompt.shape[0]
    mix = ml_norm.shape[1]
    dh = mix // ML_HEADS
    yp, ys = x_prompt, x_sample
    st_p, st_s = [], []
    for i in range(depth):
        W = _prep_weights(w_in[i], b_if[i], conv_w[i], conv_b[i], ml_norm[i], w_br_a[i],
                          w_br_b[i], w_out[i], g_pre_mix[i], g_post_mix[i], g_pre_mlp[i],
                          g_post_mlp[i], w_up[i], w_down[i], g_pre_ple[i], g_post_ple[i],
                          w_ple[i], w_ple_gate[i])
        yp, sp = _layer(yp, p_prompt[i],
                        jnp.zeros((bp, CONV_W - 1, 2 * mix), F32),
                        jnp.zeros((bp, ML_HEADS, dh, dh), F32),
                        jnp.zeros((bp, ML_HEADS, dh), F32),
                        jnp.zeros((bp, ML_HEADS), F32),
                        None, W)
        ys, ss = _layer(ys, p_sample[i], state_conv[i], state_mlstm_C[i], state_mlstm_n[i],
                        state_mlstm_m[i], (cache_sb_k[i], cache_sb_v[i]), W)
        st_p.append(sp)
        st_s.append(ss)
    stk = lambda sts, j: jnp.stack([s[j] for s in sts])
    return (yp, ys,
            stk(st_p, 0), stk(st_p, 1), stk(st_p, 2), stk(st_p, 3), stk(st_p, 4), stk(st_p, 5),
            stk(st_s, 0), stk(st_s, 1), stk(st_s, 2), stk(st_s, 3), stk(st_s, 4), stk(st_s, 5))
```
